```python
import math
import jax, jax.numpy as jnp
from jax import lax
import numpy as np

D_MODEL = 2048
BATCH = 2
SEQ = 16384
DEPTH = 2

N_MIXERS = 2
N_LRU_LAYERS = (DEPTH + 1) // 2
N_ATTN_LAYERS = DEPTH // 2
PLE_DIM = 256
NORM_EPS = 1e-6
D_RNN = D_MODEL
N_LRU_BLOCKS = 8
LRU_BLOCK = D_RNN // N_LRU_BLOCKS
CONV_WIDTH = 4
LRU_C = 8.0
DILATION_PAIRS = ((128, 1), (512, 4), (2048, 16))
N_DIL_GROUPS = len(DILATION_PAIRS)
N_HEADS = 8
HEAD_DIM = 128
ATTN_BLOCK = 128
ROPE_THETA = 10000.0
NEG_INF = -1e30
N_GROUPS = 8
EXPERTS_PER_GROUP = 8
N_EXPERTS = N_GROUPS * EXPERTS_PER_GROUP
TOP_K = 2
D_EXPERT = D_MODEL // 4
MOE_BLOCK = 128

kernel_name = "hybrid_rglru_dilated_attn_hmoe"


def rms_norm(x, gain):
    xf = x.astype(jnp.float32)
    y = xf * lax.rsqrt(jnp.mean(xf * xf, axis=-1, keepdims=True) + NORM_EPS)
    return (y * gain.astype(jnp.float32)).astype(x.dtype)


def rope_tables(positions):
    inv = ROPE_THETA ** (-jnp.arange(0, HEAD_DIM, 2, dtype=jnp.float32) / HEAD_DIM)
    ang = positions.astype(jnp.float32)[..., None] * inv
    return jnp.cos(ang), jnp.sin(ang)


def apply_rope(t, cos, sin):
    tf = t.astype(jnp.float32)
    t1, t2 = jnp.split(tf, 2, axis=-1)
    c, s = cos[:, :, None, :], sin[:, :, None, :]
    return jnp.concatenate([t1 * c - t2 * s, t2 * c + t1 * s], axis=-1).astype(t.dtype)


def causal_depthwise_conv(x, w, b):
    y = lax.conv_general_dilated(
        x, w[:, None, :].astype(x.dtype), window_strides=(1,),
        padding=[(CONV_WIDTH - 1, 0)], dimension_numbers=('NWC', 'WIO', 'NWC'),
        feature_group_count=x.shape[-1])
    return y + b.astype(x.dtype)


def rglru_mixer(hn, w_in, conv_w, conv_b, w_rg, b_rg, w_ig, b_ig, lam, w_out):
    B, S, _ = hn.shape
    proj = hn @ w_in
    gate_branch, xr = proj[..., :D_RNN], proj[..., D_RNN:]
    xr = causal_depthwise_conv(xr, conv_w, conv_b)
    xb = xr.reshape(B, S, N_LRU_BLOCKS, LRU_BLOCK)
    r = jax.nn.sigmoid((jnp.einsum('bsnc,ncd->bsnd', xb, w_rg) + b_rg).astype(jnp.float32))
    i = jax.nn.sigmoid((jnp.einsum('bsnc,ncd->bsnd', xb, w_ig) + b_ig).astype(jnp.float32))
    r = r.reshape(B, S, D_RNN)
    i = i.reshape(B, S, D_RNN)
    log_a = LRU_C * r * jax.nn.log_sigmoid(lam.astype(jnp.float32))
    a = jnp.exp(log_a)
    u = jnp.sqrt(-jnp.expm1(2.0 * log_a)) * (i * xr.astype(jnp.float32))

    def combine(left, right):
        a1, b1 = left
        a2, b2 = right
        return a1 * a2, a2 * b1 + b2

    _, h = lax.associative_scan(combine, (a, u), axis=1)
    y = jax.nn.gelu(gate_branch, approximate=True) * h.astype(hn.dtype)
    return y @ w_out


def dilated_window_attention(q, k, v, window, dilation):
    B, S, H, HD = q.shape
    span = dilation * ATTN_BLOCK
    s_pad = -(-S // span) * span
    L = s_pad // dilation
    nb = L // ATTN_BLOCK
    w_sub = window // dilation

    def to_sub(t):
        t = jnp.pad(t, ((0, 0), (0, s_pad - S), (0, 0), (0, 0)))
        t = t.reshape(B, L, dilation, H, HD).transpose(0, 2, 1, 3, 4)
        return t.reshape(B * dilation, nb, ATTN_BLOCK, H, HD)

    qs, ks, vs = to_sub(q), to_sub(k), to_sub(v)

    def with_prev(t):
        prev = jnp.concatenate([jnp.zeros_like(t[:, :1]), t[:, :-1]], axis=1)
        return jnp.concatenate([prev, t], axis=2)

    k2, v2 = with_prev(ks), with_prev(vs)
    scores = jnp.einsum('bnqhd,bnkhd->bnhqk', qs, k2).astype(jnp.float32) * (HD ** -0.5)
    qpos = jnp.arange(ATTN_BLOCK) + ATTN_BLOCK
    kpos = jnp.arange(2 * ATTN_BLOCK)
    rel = qpos[:, None] - kpos[None, :]
    key_sub = jnp.arange(nb)[:, None] * ATTN_BLOCK - ATTN_BLOCK + kpos[None, :]
    valid = ((rel >= 0) & (rel <= w_sub))[None] & (key_sub >= 0)[:, None, :]
    scores = jnp.where(valid[None, :, None], scores, NEG_INF)
    m = jnp.max(scores, axis=-1, keepdims=True)
    pr = jnp.exp(scores - m)
    den = jnp.sum(pr, axis=-1)
    o = jnp.einsum('bnhqk,bnkhd->bnqhd', pr, v2.astype(jnp.float32))
    o = o / jnp.transpose(den, (0, 1, 3, 2))[..., None]
    lse = jnp.transpose(m[..., 0] + jnp.log(den), (0, 1, 3, 2))
    o = o.reshape(B, dilation, L, H, HD).transpose(0, 2, 1, 3, 4).reshape(B, s_pad, H, HD)[:, :S]
    lse = lse.reshape(B, dilation, L, H).transpose(0, 2, 1, 3).reshape(B, s_pad, H)[:, :S]
    return o, lse


def dilated_attention_mixer(hn, cos, sin, w_qkv, q_gain, k_gain, w_o):
    B, S, _ = hn.shape
    qkv = (hn @ w_qkv).reshape(B, S, N_DIL_GROUPS, 3, N_HEADS, HEAD_DIM)
    outs, lses = [], []
    for g, (window, dilation) in enumerate(DILATION_PAIRS):
        q = apply_rope(rms_norm(qkv[:, :, g, 0], q_gain[g]), cos, sin)
        k = apply_rope(rms_norm(qkv[:, :, g, 1], k_gain[g]), cos, sin)
        o, lse = dilated_window_attention(q, k, qkv[:, :, g, 2], window, dilation)
        outs.append(o)
        lses.append(lse)
    wts = jax.nn.softmax(jnp.stack(lses), axis=0)
    o = jnp.einsum('gbsh,gbshd->bshd', wts, jnp.stack(outs))
    return o.reshape(B, S, N_HEADS * HEAD_DIM).astype(hn.dtype) @ w_o


def hierarchical_moe(hn, w_group, b_group, w_expert, b_expert, w_gate, w_up, w_down):
    B, S, D = hn.shape
    T = B * S
    xf = hn.reshape(T, D)
    tok = jnp.arange(T)
    lg = (xf @ w_group + b_group).astype(jnp.float32)
    pg = jax.nn.softmax(lg, axis=-1)
    g_top = jnp.argmax(lg, axis=-1)
    p_top = pg[tok, g_top]
    le_all = (jnp.einsum('td,gde->tge', xf, w_expert) + b_expert).astype(jnp.float32)
    le = le_all[tok, g_top]
    top_v, top_i = lax.top_k(le, TOP_K)
    gates = p_top[:, None] * jax.nn.softmax(top_v, axis=-1)
    eidx = (g_top[:, None] * EXPERTS_PER_GROUP + top_i).astype(jnp.int32)

    A = T * TOP_K
    flat_e = eidx.reshape(A)
    flat_tok = (jnp.arange(A, dtype=jnp.int32) // TOP_K)
    flat_gate = gates.reshape(A)
    order = jnp.argsort(flat_e)
    se = flat_e[order]
    counts = jnp.zeros((N_EXPERTS,), jnp.int32).at[flat_e].add(1)
    padded = (counts + MOE_BLOCK - 1) // MOE_BLOCK * MOE_BLOCK
    pad_end = jnp.cumsum(padded)
    pad_start = pad_end - padded
    start = jnp.cumsum(counts) - counts
    dest = pad_start[se] + jnp.arange(A, dtype=jnp.int32) - start[se]
    n_blocks = (A + N_EXPERTS * (MOE_BLOCK - 1) + MOE_BLOCK - 1) // MOE_BLOCK
    P = n_blocks * MOE_BLOCK
    slot_tok = jnp.zeros((P,), jnp.int32).at[dest].set(flat_tok[order])
    slot_gate = jnp.zeros((P,), jnp.float32).at[dest].set(flat_gate[order])
    block_e = jnp.minimum(
        jnp.searchsorted(pad_end, jnp.arange(n_blocks, dtype=jnp.int32) * MOE_BLOCK, side='right'),
        N_EXPERTS - 1)
    xin = xf[slot_tok].reshape(n_blocks, MOE_BLOCK, D)

    def expert_block(args):
        xb, e = args
        return (jax.nn.silu(xb @ w_gate[e]) * (xb @ w_up[e])) @ w_down[e]

    yb = lax.map(expert_block, (xin, block_e)).reshape(P, D)
    out = jnp.zeros((T, D), hn.dtype).at[slot_tok].add(yb * slot_gate[:, None].astype(yb.dtype))
    return out.reshape(B, S, D)


def per_layer_embedding(h, p_i, w_proj, w_gate, gain):
    e = p_i @ w_proj
    g = jax.nn.sigmoid((h @ w_gate).astype(jnp.float32)).astype(h.dtype)
    return rms_norm(e * g, gain)


def setup_inputs(seed: int = 0) -> dict:
    key = jax.random.key(seed)
    ks = iter(jax.random.split(key, 40))
    f32 = jnp.float32

    def nrm(shape, fan_in):
        return jax.random.normal(next(ks), shape, f32) * (fan_in ** -0.5)

    def gain(shape):
        return 1.0 + 0.05 * jax.random.normal(next(ks), shape, f32)

    def bias(shape, scale=0.02):
        return scale * jax.random.normal(next(ks), shape, f32)

    x = jax.random.normal(next(ks), (BATCH, SEQ, D_MODEL), f32)
    p = jax.random.normal(next(ks), (DEPTH, BATCH, SEQ, PLE_DIM), f32)
    offs = jax.random.randint(next(ks), (BATCH, 1), 0, 1024, dtype=jnp.int32)
    positions = (jnp.arange(SEQ, dtype=jnp.int32)[None, :] + offs).astype(jnp.int32)

    u = jax.random.uniform(next(ks), (N_LRU_LAYERS, D_RNN), f32, 0.9, 0.999)
    a_base = u ** (1.0 / LRU_C)
    lru_lambda = jnp.log(a_base) - jnp.log1p(-a_base)

    return {
        "x": x, "p": p, "positions": positions,
        "norm_mix": gain((DEPTH, D_MODEL)),
        "norm_ffn": gain((DEPTH, D_MODEL)),
        "lru_w_in": nrm((N_LRU_LAYERS, D_MODEL, 2 * D_RNN), D_MODEL),
        "lru_conv_w": nrm((N_LRU_LAYERS, CONV_WIDTH, D_RNN), CONV_WIDTH),
        "lru_conv_b": bias((N_LRU_LAYERS, D_RNN)),
        "lru_w_rg": nrm((N_LRU_LAYERS, N_LRU_BLOCKS, LRU_BLOCK, LRU_BLOCK), LRU_BLOCK),
        "lru_b_rg": bias((N_LRU_LAYERS, N_LRU_BLOCKS, LRU_BLOCK)),
        "lru_w_ig": nrm((N_LRU_LAYERS, N_LRU_BLOCKS, LRU_BLOCK, LRU_BLOCK), LRU_BLOCK),
        "lru_b_ig": bias((N_LRU_LAYERS, N_LRU_BLOCKS, LRU_BLOCK)),
        "lru_lambda": lru_lambda,
        "lru_w_out": nrm((N_LRU_LAYERS, D_RNN, D_MODEL), D_RNN),
        "attn_w_qkv": nrm((N_ATTN_LAYERS, D_MODEL, N_DIL_GROUPS * 3 * N_HEADS * HEAD_DIM), D_MODEL),
        "attn_q_gain": gain((N_ATTN_LAYERS, N_DIL_GROUPS, HEAD_DIM)),
        "attn_k_gain": gain((N_ATTN_LAYERS, N_DIL_GROUPS, HEAD_DIM)),
        "attn_w_o": nrm((N_ATTN_LAYERS, N_HEADS * HEAD_DIM, D_MODEL), N_HEADS * HEAD_DIM),
        "moe_w_group": nrm((DEPTH, D_MODEL, N_GROUPS), D_MODEL),
        "moe_b_group": bias((DEPTH, N_GROUPS), 0.01),
        "moe_w_expert": nrm((DEPTH, N_GROUPS, D_MODEL, EXPERTS_PER_GROUP), D_MODEL),
        "moe_b_expert": bias((DEPTH, N_GROUPS, EXPERTS_PER_GROUP), 0.01),
        "moe_w_gate": nrm((DEPTH, N_EXPERTS, D_MODEL, D_EXPERT), D_MODEL),
        "moe_w_up": nrm((DEPTH, N_EXPERTS, D_MODEL, D_EXPERT), D_MODEL),
        "moe_w_down": nrm((DEPTH, N_EXPERTS, D_EXPERT, D_MODEL), D_EXPERT),
        "ple_w_proj": nrm((DEPTH, PLE_DIM, D_MODEL), PLE_DIM),
        "ple_w_gate": nrm((DEPTH, D_MODEL, D_MODEL), D_MODEL),
        "ple_norm": gain((DEPTH, D_MODEL)),
    }


def reference(x, p, positions, norm_mix, norm_ffn,
              lru_w_in, lru_conv_w, lru_conv_b, lru_w_rg, lru_b_rg, lru_w_ig, lru_b_ig,
              lru_lambda, lru_w_out,
              attn_w_qkv, attn_q_gain, attn_k_gain, attn_w_o,
              moe_w_group, moe_b_group, moe_w_expert, moe_b_expert,
              moe_w_gate, moe_w_up, moe_w_down,
              ple_w_proj, ple_w_gate, ple_norm):
    cos, sin = rope_tables(positions)
    h = x
    for i in range(DEPTH):
        j = i // N_MIXERS
        hn = rms_norm(h, norm_mix[i])
        if i % N_MIXERS == 0:
            mix = rglru_mixer(hn, lru_w_in[j], lru_conv_w[j], lru_conv_b[j],
                              lru_w_rg[j], lru_b_rg[j], lru_w_ig[j], lru_b_ig[j],
                              lru_lambda[j], lru_w_out[j])
        else:
            mix = dilated_attention_mixer(hn, cos, sin, attn_w_qkv[j], attn_q_gain[j],
                                          attn_k_gain[j], attn_w_o[j])
        h = h + mix
        h = h + hierarchical_moe(rms_norm(h, norm_ffn[i]), moe_w_group[i], moe_b_group[i],
                                 moe_w_expert[i], moe_b_expert[i],
                                 moe_w_gate[i], moe_w_up[i], moe_w_down[i])
        h = h + per_layer_embedding(h, p[i].astype(h.dtype), ple_w_proj[i], ple_w_gate[i], ple_norm[i])
    return h
```

```python
import functools
import math

import jax
import jax.numpy as jnp
from jax import lax
from jax.experimental import pallas as pl
from jax.experimental.pallas import tpu as pltpu

NORM_EPS = 1e-6
N_LRU_BLOCKS = 8
CONV_WIDTH = 4
LRU_C = 8.0
DILATION_PAIRS = ((128, 1), (512, 4), (2048, 16))
N_HEADS = 8
HEAD_DIM = 128
ATTN_BLOCK = 128
ROPE_THETA = 10000.0
NEG_INF = -1e30
N_GROUPS = 8
EXPERTS_PER_GROUP = 8
N_EXPERTS = N_GROUPS * EXPERTS_PER_GROUP
TOP_K = 2

LANES = 128
SUBLANES = 8
VMEM_LIMIT_BYTES = 56 * 1024 * 1024

ROW_TILE = 512
LRU_CHUNK = 256
ATTN_TILE = 512
MOE_BLOCK = 256
DISPATCH_TILE = 256

F32 = jnp.float32
BF16 = jnp.bfloat16


def _cparams(*sem):
    return pltpu.CompilerParams(dimension_semantics=sem, vmem_limit_bytes=VMEM_LIMIT_BYTES)


def _rms(x, gain):
    ms = jnp.mean(x * x, axis=-1, keepdims=True)
    return x * lax.rsqrt(ms + NORM_EPS) * gain


def _norm_matmul_kernel(x_ref, g_ref, w_ref, o_ref, xn_ref):
    @pl.when(pl.program_id(1) == 0)
    def _():
        xn_ref[...] = _rms(x_ref[...], g_ref[...]).astype(BF16)

    o_ref[...] = jnp.dot(xn_ref[...], w_ref[...], preferred_element_type=F32).astype(o_ref.dtype)


def norm_matmul(x, gain, w, *, tn, out_dtype=F32):
    t, k = x.shape
    n = w.shape[1]
    tm = min(ROW_TILE, t)
    return pl.pallas_call(
        _norm_matmul_kernel,
        grid=(t // tm, n // tn),
        in_specs=[
            pl.BlockSpec((tm, k), lambda i, j: (i, 0)),
            pl.BlockSpec((1, k), lambda i, j: (0, 0)),
            pl.BlockSpec((k, tn), lambda i, j: (0, j)),
        ],
        out_specs=pl.BlockSpec((tm, tn), lambda i, j: (i, j)),
        out_shape=jax.ShapeDtypeStruct((t, n), out_dtype),
        scratch_shapes=[pltpu.VMEM((tm, k), BF16)],
        compiler_params=_cparams("parallel", "arbitrary"),
        name="norm_matmul",
    )(x, gain.reshape(1, k), w)


def _shift_rows_in_vreg(x, s, fill):
    g = x.shape[0] // SUBLANES
    x3 = x.reshape(g, SUBLANES, x.shape[1])
    rolled = pltpu.roll(x3, s, axis=1)
    row = lax.broadcasted_iota(jnp.int32, x3.shape, 1)
    return jnp.where(row >= s, rolled, fill).reshape(x.shape)


def _one_minus_exp(x, ex):
    near = jnp.where(ex == 1.0, -x, (1.0 - ex) * x / jnp.log(ex))
    return jnp.where(x > -0.5, near, 1.0 - ex)


def _lru_kernel(gate_ref, xr_ref, hres_ref, cw_ref, cb_ref, wg_ref, brg_ref, big_ref, lam_ref, wo_ref,
                o_ref, xprev_ref, hcarry_ref, xc_ref, a_ref, u_ref):
    tc, d = xr_ref.shape
    blk = d // N_LRU_BLOCKS

    @pl.when(pl.program_id(1) == 0)
    def _():
        xprev_ref[...] = jnp.zeros_like(xprev_ref)
        hcarry_ref[...] = jnp.zeros_like(hcarry_ref)

    x = xr_ref[...]
    prev = xprev_ref[...]
    cw = cw_ref[...]
    acc = cb_ref[...] + cw[CONV_WIDTH - 1:CONV_WIDTH] * x
    head = cb_ref[...] + cw[CONV_WIDTH - 1:CONV_WIDTH] * x[0:SUBLANES]
    row8 = lax.broadcasted_iota(jnp.int32, (SUBLANES, d), 0)
    for s in range(1, CONV_WIDTH):
        wk = cw[CONV_WIDTH - 1 - s:CONV_WIDTH - s]
        acc = acc + wk * pltpu.roll(x, s, axis=0)
        first = jnp.where(row8 >= s, pltpu.roll(x[0:SUBLANES], s, axis=0), pltpu.roll(prev, s, axis=0))
        head = head + wk * first
    xc_ref[...] = acc
    xc_ref[0:SUBLANES, :] = head
    xprev_ref[...] = x[tc - SUBLANES:tc]

    lam = lam_ref[...]
    log_sig = jnp.minimum(lam, 0.0) - jnp.log1p(jnp.exp(-jnp.abs(lam)))

    for n in range(N_LRU_BLOCKS):
        sl = slice(n * blk, (n + 1) * blk)
        xc = xc_ref[:, sl]
        z = jnp.dot(xc.astype(BF16), wg_ref[n], preferred_element_type=F32)
        r = jax.nn.sigmoid(z[:, :blk] + brg_ref[:, sl])
        i = jax.nn.sigmoid(z[:, blk:] + big_ref[:, sl])
        log_a = LRU_C * r * log_sig[:, sl]
        a = jnp.exp(log_a)
        a_ref[:, sl] = a
        u_ref[:, sl] = jnp.sqrt(_one_minus_exp(2.0 * log_a, a * a)) * (i * xc)

    a = a_ref[...]
    u = u_ref[...]
    for s in (1, 2, 4):
        a_sh = _shift_rows_in_vreg(a, s, 1.0)
        u_sh = _shift_rows_in_vreg(u, s, 0.0)
        u = a * u_sh + u
        a = a * a_sh
    h = hcarry_ref[...]
    for j in range(tc // SUBLANES):
        rows = slice(j * SUBLANES, (j + 1) * SUBLANES)
        hj = a[rows] * h + u[rows]
        u_ref[rows, :] = hj
        h = hj[SUBLANES - 1:SUBLANES]
    hcarry_ref[...] = h

    y = jax.nn.gelu(gate_ref[...], approximate=True) * u_ref[...]
    o_ref[...] = hres_ref[...] + jnp.dot(y.astype(BF16), wo_ref[...], preferred_element_type=F32)


def lru_core(proj, h_res, conv_w, conv_b, w_gates, b_rg, b_ig, lam, w_out, *, batch):
    t, d = h_res.shape
    seq = t // batch
    tc = min(LRU_CHUNK, seq)
    nc = seq // tc
    blk = d // N_LRU_BLOCKS
    row = lambda b, c: (b * nc + c, 0)
    const2 = lambda b, c: (0, 0)
    return pl.pallas_call(
        _lru_kernel,
        grid=(batch, nc),
        in_specs=[
            pl.BlockSpec((tc, d), lambda b, c: (b * nc + c, 0)),
            pl.BlockSpec((tc, d), lambda b, c: (b * nc + c, 1)),
            pl.BlockSpec((tc, d), row),
            pl.BlockSpec((CONV_WIDTH, d), const2),
            pl.BlockSpec((1, d), const2),
            pl.BlockSpec((N_LRU_BLOCKS, blk, 2 * blk), lambda b, c: (0, 0, 0)),
            pl.BlockSpec((1, d), const2),
            pl.BlockSpec((1, d), const2),
            pl.BlockSpec((1, d), const2),
            pl.BlockSpec((d, d), const2),
        ],
        out_specs=pl.BlockSpec((tc, d), row),
        out_shape=jax.ShapeDtypeStruct((t, d), F32),
        scratch_shapes=[
            pltpu.VMEM((SUBLANES, d), F32),
            pltpu.VMEM((1, d), F32),
            pltpu.VMEM((tc, d), F32),
            pltpu.VMEM((tc, d), F32),
            pltpu.VMEM((tc, d), F32),
        ],
        compiler_params=_cparams("parallel", "arbitrary"),
        name="lru_core",
    )(proj, proj, h_res, conv_w, conv_b.reshape(1, d), w_gates, b_rg.reshape(1, d), b_ig.reshape(1, d),
      lam.reshape(1, d), w_out)


def _rope_kernel(pos_ref, inv_ref, cos_ref, sin_ref):
    ang = pos_ref[...].astype(F32) * inv_ref[...]
    lane = lax.broadcasted_iota(jnp.int32, ang.shape, 1)
    cos_ref[...] = jnp.cos(ang)
    sin_ref[...] = jnp.where(lane < HEAD_DIM // 2, -jnp.sin(ang), jnp.sin(ang))


def rope_tables(positions_flat):
    t = positions_flat.shape[0]
    tm = min(ROW_TILE, t)
    inv = ROPE_THETA ** (-jnp.arange(0, HEAD_DIM, 2, dtype=F32) / HEAD_DIM)
    inv2 = jnp.concatenate([inv, inv]).reshape(1, HEAD_DIM)
    return pl.pallas_call(
        _rope_kernel,
        grid=(t // tm,),
        in_specs=[pl.BlockSpec((tm, 1), lambda i: (i, 0)), pl.BlockSpec((1, HEAD_DIM), lambda i: (0, 0))],
        out_specs=[pl.BlockSpec((tm, HEAD_DIM), lambda i: (i, 0))] * 2,
        out_shape=[jax.ShapeDtypeStruct((t, HEAD_DIM), F32)] * 2,
        compiler_params=_cparams("parallel"),
        name="rope_tables",
    )(positions_flat.reshape(t, 1), inv2)


def _qkv_kernel(x_ref, g_ref, w_ref, cos_ref, sin_ref, hg_ref, o_ref, xn_ref):
    j = pl.program_id(1)

    @pl.when(j == 0)
    def _():
        xn_ref[...] = _rms(x_ref[...], g_ref[...]).astype(BF16)

    acc = jnp.dot(xn_ref[...], w_ref[...], preferred_element_type=F32)
    kind = j % 3

    @pl.when(kind == 2)
    def _():
        o_ref[...] = acc.astype(o_ref.dtype)

    @pl.when(kind != 2)
    def _():
        cos = cos_ref[...]
        sin = sin_ref[...]
        gain = hg_ref[...]
        scale = jnp.where(kind == 0, HEAD_DIM ** -0.5, 1.0).astype(F32)
        for h in range(N_HEADS):
            sl = slice(h * HEAD_DIM, (h + 1) * HEAD_DIM)
            y = _rms(acc[:, sl], gain)
            y = y * cos + pltpu.roll(y, HEAD_DIM // 2, axis=1) * sin
            o_ref[:, sl] = (y * scale).astype(o_ref.dtype)


def qkv_project(x, gain, w_qkv, cos, sin, head_gains):
    t, k = x.shape
    n = w_qkv.shape[1]
    tn = N_HEADS * HEAD_DIM
    tm = min(ROW_TILE, t)
    return pl.pallas_call(
        _qkv_kernel,
        grid=(t // tm, n // tn),
        in_specs=[
            pl.BlockSpec((tm, k), lambda i, j: (i, 0)),
            pl.BlockSpec((1, k), lambda i, j: (0, 0)),
            pl.BlockSpec((k, tn), lambda i, j: (0, j)),
            pl.BlockSpec((tm, HEAD_DIM), lambda i, j: (i, 0)),
            pl.BlockSpec((tm, HEAD_DIM), lambda i, j: (i, 0)),
            pl.BlockSpec((None, 1, HEAD_DIM), lambda i, j: (j, 0, 0)),
        ],
        out_specs=pl.BlockSpec((tm, tn), lambda i, j: (i, j)),
        out_shape=jax.ShapeDtypeStruct((t, n), BF16),
        scratch_shapes=[pltpu.VMEM((tm, k), BF16)],
        compiler_params=_cparams("parallel", "arbitrary"),
        name="qkv_project",
    )(x, gain.reshape(1, k), w_qkv, cos, sin, head_gains)


def _attn_kernel(q_ref, kc_ref, kp_ref, vc_ref, vp_ref, o_ref, lse_ref, kbuf_ref, vbuf_ref):
    tl = q_ref.shape[0]
    blk = ATTN_BLOCK
    first_tile = pl.program_id(2) == 0

    kbuf_ref[0:blk, :] = kp_ref[...]
    kbuf_ref[blk:, :] = kc_ref[...]
    vbuf_ref[0:blk, :] = vp_ref[...]
    vbuf_ref[blk:, :] = vc_ref[...]

    qi = lax.broadcasted_iota(jnp.int32, (blk, 2 * blk), 0)
    kj = lax.broadcasted_iota(jnp.int32, (blk, 2 * blk), 1)
    upper = kj <= qi + blk
    bias_gen = jnp.where(upper & (kj >= qi), 0.0, NEG_INF).astype(F32)
    bias_first = jnp.where(upper & (kj >= blk), 0.0, NEG_INF).astype(F32)
    bias0 = jnp.where(first_tile, bias_first, bias_gen)
    lane = lax.broadcasted_iota(jnp.int32, (blk, LANES), 1)

    for qb in range(tl // blk):
        bias = bias0 if qb == 0 else bias_gen
        rows = slice(qb * blk, (qb + 1) * blk)
        win = slice(qb * blk, (qb + 2) * blk)
        lse_tile = jnp.zeros((blk, LANES), F32)
        for h in range(N_HEADS):
            cols = slice(h * HEAD_DIM, (h + 1) * HEAD_DIM)
            s = lax.dot_general(q_ref[rows, cols], kbuf_ref[win, cols], (((1,), (1,)), ((), ())),
                                preferred_element_type=F32) + bias
            m = jnp.max(s, axis=-1, keepdims=True)
            p = jnp.exp(s - m)
            den = jnp.sum(p, axis=-1, keepdims=True)
            o = jnp.dot(p.astype(BF16), vbuf_ref[win, cols], preferred_element_type=F32)
            o_ref[rows, cols] = o / den
            lse_tile = jnp.where(lane == h, m + jnp.log(den), lse_tile)
        lse_ref[rows, :] = lse_tile


def dilated_attention(qkv, group, dilation, *, batch):
    t, n = qkv.shape
    seq = t // batch
    sub = seq // dilation
    tl = min(ATTN_TILE, sub)
    hd = N_HEADS * HEAD_DIM
    ncol = n // hd
    view = qkv.reshape(batch, sub, dilation * n)
    per = tl // ATTN_BLOCK
    col = lambda kind: (lambda b, r, i: (b, i, r * ncol + group * 3 + kind))
    col_prev = lambda kind: (lambda b, r, i: (b, jnp.maximum(i * per - 1, 0), r * ncol + group * 3 + kind))
    o, lse = pl.pallas_call(
        _attn_kernel,
        grid=(batch, dilation, sub // tl),
        in_specs=[
            pl.BlockSpec((None, tl, hd), col(0)),
            pl.BlockSpec((None, tl, hd), col(1)),
            pl.BlockSpec((None, ATTN_BLOCK, hd), col_prev(1)),
            pl.BlockSpec((None, tl, hd), col(2)),
            pl.BlockSpec((None, ATTN_BLOCK, hd), col_prev(2)),
        ],
        out_specs=[
            pl.BlockSpec((None, tl, hd), lambda b, r, i: (b, i, r)),
            pl.BlockSpec((None, tl, LANES), lambda b, r, i: (b, i, r)),
        ],
        out_shape=[
            jax.ShapeDtypeStruct((batch, sub, dilation * hd), F32),
            jax.ShapeDtypeStruct((batch, sub, dilation * LANES), F32),
        ],
        scratch_shapes=[pltpu.VMEM((tl + ATTN_BLOCK, hd), BF16), pltpu.VMEM((tl + ATTN_BLOCK, hd), BF16)],
        compiler_params=_cparams("parallel", "parallel", "arbitrary"),
        name=f"dilated_attention_d{dilation}",
    )(view, view, view, view, view)
    return o.reshape(t, hd), lse.reshape(t, LANES)


def _merge_kernel(o0_ref, o1_ref, o2_ref, l0_ref, l1_ref, l2_ref, hres_ref, wo_ref, out_ref):
    l0, l1, l2 = l0_ref[...], l1_ref[...], l2_ref[...]
    m = jnp.maximum(jnp.maximum(l0, l1), l2)
    e0, e1, e2 = jnp.exp(l0 - m), jnp.exp(l1 - m), jnp.exp(l2 - m)
    tot = e0 + e1 + e2
    w0, w1, w2 = e0 / tot, e1 / tot, e2 / tot
    tm = l0.shape[0]
    parts = []
    for h in range(N_HEADS):
        cols = slice(h * HEAD_DIM, (h + 1) * HEAD_DIM)
        bc = lambda w: jnp.broadcast_to(w[:, h:h + 1], (tm, HEAD_DIM))
        parts.append(bc(w0) * o0_ref[:, cols] + bc(w1) * o1_ref[:, cols] + bc(w2) * o2_ref[:, cols])
    o = jnp.concatenate(parts, axis=1).astype(BF16)
    out_ref[...] = hres_ref[...] + jnp.dot(o, wo_ref[...], preferred_element_type=F32)


def merge_project(outs, lses, h_res, w_o):
    t, d = h_res.shape
    hd = N_HEADS * HEAD_DIM
    tm = min(ROW_TILE, t)
    row = lambda i: (i, 0)
    return pl.pallas_call(
        _merge_kernel,
        grid=(t // tm,),
        in_specs=[pl.BlockSpec((tm, hd), row)] * 3 + [pl.BlockSpec((tm, LANES), row)] * 3 + [
            pl.BlockSpec((tm, d), row),
            pl.BlockSpec((hd, d), lambda i: (0, 0)),
        ],
        out_specs=pl.BlockSpec((tm, d), row),
        out_shape=jax.ShapeDtypeStruct((t, d), F32),
        compiler_params=_cparams("parallel"),
        name="merge_project",
    )(*outs, *lses, h_res, w_o)


def _router_kernel(h_ref, g_ref, wr_ref, br_ref, hn_ref, eidx_ref, gate_ref, rank_ref, cnt_ref, carry_ref):
    tm = h_ref.shape[0]
    half = h_ref.shape[1] // 2

    @pl.when(pl.program_id(0) == 0)
    def _():
        carry_ref[...] = jnp.zeros_like(carry_ref)

    hn = _rms(h_ref[...], g_ref[...])
    lo = pltpu.bitcast(hn[:, :half].astype(BF16).astype(F32), jnp.uint32)
    hi = pltpu.bitcast(hn[:, half:].astype(BF16).astype(F32), jnp.uint32)
    hn_ref[...] = hi | (lo >> 16)

    logits = lax.dot_general(wr_ref[...], hn, (((1,), (1,)), ((), ())), precision=lax.Precision.HIGHEST,
                             preferred_element_type=F32) + br_ref[:, 0:1]
    lg = logits[0:N_GROUPS]
    gi = lax.broadcasted_iota(jnp.int32, (N_GROUPS, tm), 0)
    m = jnp.max(lg, axis=0, keepdims=True)
    p_top = 1.0 / jnp.sum(jnp.exp(lg - m), axis=0, keepdims=True)
    g_top = jnp.min(jnp.where(lg == m, gi, N_GROUPS), axis=0, keepdims=True)

    le = jnp.zeros((EXPERTS_PER_GROUP, tm), F32)
    for g in range(N_GROUPS):
        lo_row = N_GROUPS + g * EXPERTS_PER_GROUP
        le = jnp.where(g_top == g, logits[lo_row:lo_row + EXPERTS_PER_GROUP], le)
    ei = lax.broadcasted_iota(jnp.int32, (EXPERTS_PER_GROUP, tm), 0)
    v1 = jnp.max(le, axis=0, keepdims=True)
    i1 = jnp.min(jnp.where(le == v1, ei, EXPERTS_PER_GROUP), axis=0, keepdims=True)
    le2 = jnp.where(ei == i1, -jnp.inf, le)
    v2 = jnp.max(le2, axis=0, keepdims=True)
    i2 = jnp.min(jnp.where(le2 == v2, ei, EXPERTS_PER_GROUP), axis=0, keepdims=True)
    e21 = jnp.exp(v2 - v1)
    gate1 = p_top / (1.0 + e21)
    gate2 = p_top * e21 / (1.0 + e21)
    e1 = g_top * EXPERTS_PER_GROUP + i1
    e2 = g_top * EXPERTS_PER_GROUP + i2
    eidx_ref[...] = jnp.concatenate([e1, e2], axis=0)
    gate_ref[...] = jnp.concatenate([gate1, gate2], axis=0)

    ee = lax.broadcasted_iota(jnp.int32, (N_EXPERTS, tm), 0)
    oh1 = ee == e1
    oh2 = ee == e2
    onehot = jnp.where(oh1 | oh2, 1.0, 0.0)
    tri = jnp.where(lax.broadcasted_iota(jnp.int32, (tm, tm), 0) < lax.broadcasted_iota(jnp.int32, (tm, tm), 1),
                    1.0, 0.0).astype(BF16)
    before = jnp.dot(onehot.astype(BF16), tri, preferred_element_type=F32) + carry_ref[:, 0:1]
    r1 = jnp.sum(jnp.where(oh1, before, 0.0), axis=0, keepdims=True)
    r2 = jnp.sum(jnp.where(oh2, before, 0.0), axis=0, keepdims=True)
    rank_ref[...] = jnp.concatenate([r1, r2], axis=0).astype(jnp.int32)
    carry_ref[...] = carry_ref[...] + jnp.sum(onehot, axis=1, keepdims=True)
    cnt_ref[...] = carry_ref[...].astype(jnp.int32)


def moe_router(h, gain, w_router_t, b_router):
    t, d = h.shape
    tm = min(ROW_TILE, t)
    nlog = w_router_t.shape[0]
    return pl.pallas_call(
        _router_kernel,
        grid=(t // tm,),
        in_specs=[
            pl.BlockSpec((tm, d), lambda i: (i, 0)),
            pl.BlockSpec((1, d), lambda i: (0, 0)),
            pl.BlockSpec((nlog, d), lambda i: (0, 0)),
            pl.BlockSpec((nlog, LANES), lambda i: (0, 0)),
        ],
        out_specs=[
            pl.BlockSpec((tm, d // 2), lambda i: (i, 0)),
            pl.BlockSpec((TOP_K, tm), lambda i: (0, i)),
            pl.BlockSpec((TOP_K, tm), lambda i: (0, i)),
            pl.BlockSpec((TOP_K, tm), lambda i: (0, i)),
            pl.BlockSpec((N_EXPERTS, LANES), lambda i: (0, 0)),
        ],
        out_shape=[
            jax.ShapeDtypeStruct((t, d // 2), jnp.uint32),
            jax.ShapeDtypeStruct((TOP_K, t), jnp.int32),
            jax.ShapeDtypeStruct((TOP_K, t), F32),
            jax.ShapeDtypeStruct((TOP_K, t), jnp.int32),
            jax.ShapeDtypeStruct((N_EXPERTS, LANES), jnp.int32),
        ],
        scratch_shapes=[pltpu.VMEM((N_EXPERTS, LANES), F32)],
        compiler_params=_cparams("arbitrary"),
        name="moe_router",
    )(h, gain.reshape(1, d), w_router_t, b_router)


def _scatter_kernel(dest_ref, x_ref, init_ref, o_ref, sem):
    del init_ref
    tm = x_ref.shape[0]
    t = dest_ref.shape[0] // TOP_K
    base = pl.program_id(0) * tm

    def row_copy(r, k):
        d = dest_ref[k * t + base + r]
        return pltpu.make_async_copy(x_ref.at[pl.ds(r, 1)], o_ref.at[pl.ds(d, 1)], sem)

    def start(r, c):
        for k in range(TOP_K):
            row_copy(r, k).start()
        return c

    def wait(r, c):
        for k in range(TOP_K):
            row_copy(r, k).wait()
        return c

    lax.fori_loop(0, tm, start, 0)
    lax.fori_loop(0, tm, wait, 0)


def moe_scatter(dest_flat, hn_packed, n_slots):
    t, w = hn_packed.shape
    tm = min(DISPATCH_TILE, t)
    init = jnp.zeros((n_slots, w), hn_packed.dtype)
    return pl.pallas_call(
        _scatter_kernel,
        grid_spec=pltpu.PrefetchScalarGridSpec(
            num_scalar_prefetch=1,
            grid=(t // tm,),
            in_specs=[pl.BlockSpec((tm, w), lambda i, dest: (i, 0)), pl.BlockSpec(memory_space=pl.ANY)],
            out_specs=pl.BlockSpec(memory_space=pl.ANY),
            scratch_shapes=[pltpu.SemaphoreType.DMA(())],
        ),
        out_shape=jax.ShapeDtypeStruct((n_slots, w), hn_packed.dtype),
        input_output_aliases={2: 0},
        compiler_params=_cparams("arbitrary"),
        name="moe_scatter",
    )(dest_flat, hn_packed, init)


def _expert_kernel(be_ref, x_ref, wg_ref, wu_ref, wd_ref, o_ref):
    del be_ref
    xw = x_ref[...]
    lo = pltpu.bitcast(xw << 16, F32).astype(BF16)
    hi = pltpu.bitcast(xw & jnp.uint32(0xFFFF0000), F32).astype(BF16)
    x = jnp.concatenate([lo, hi], axis=1)
    g = jnp.dot(x, wg_ref[...], preferred_element_type=F32)
    u = jnp.dot(x, wu_ref[...], preferred_element_type=F32)
    a = (g * jax.nn.sigmoid(g) * u).astype(BF16)
    o_ref[...] = jnp.dot(a, wd_ref[...], preferred_element_type=F32)


def moe_experts(block_e, xin, w_gate, w_up, w_down):
    p, w = xin.shape
    d = 2 * w
    de = w_gate.shape[2]
    return pl.pallas_call(
        _expert_kernel,
        grid_spec=pltpu.PrefetchScalarGridSpec(
            num_scalar_prefetch=1,
            grid=(p // MOE_BLOCK,),
            in_specs=[
                pl.BlockSpec((MOE_BLOCK, w), lambda i, be: (i, 0)),
                pl.BlockSpec((None, d, de), lambda i, be: (be[i], 0, 0)),
                pl.BlockSpec((None, d, de), lambda i, be: (be[i], 0, 0)),
                pl.BlockSpec((None, de, d), lambda i, be: (be[i], 0, 0)),
            ],
            out_specs=pl.BlockSpec((MOE_BLOCK, d), lambda i, be: (i, 0)),
        ),
        out_shape=jax.ShapeDtypeStruct((p, d), F32),
        compiler_params=_cparams("arbitrary"),
        name="moe_experts",
    )(block_e, xin, w_gate, w_up, w_down)


def _combine_kernel(dest_ref, h_ref, gates_ref, p_ref, y_hbm, wpg_ref, wpp_ref, pn_ref, o_ref, ybuf, sems):
    tm = h_ref.shape[0]
    t = dest_ref.shape[0] // TOP_K
    i = pl.program_id(0)
    n = pl.num_programs(0)

    def row_copy(tile, slot, r, k):
        d = dest_ref[k * t + tile * tm + r]
        return pltpu.make_async_copy(y_hbm.at[pl.ds(d, 1)], ybuf.at[slot, k, pl.ds(r, 1)], sems.at[slot])

    def start_tile(tile, slot):
        def body(r, c):
            for k in range(TOP_K):
                row_copy(tile, slot, r, k).start()
            return c
        lax.fori_loop(0, tm, body, 0)

    def wait_tile(tile, slot):
        def body(r, c):
            for k in range(TOP_K):
                row_copy(tile, slot, r, k).wait()
            return c
        lax.fori_loop(0, tm, body, 0)

    slot = i % 2

    @pl.when(i == 0)
    def _():
        start_tile(0, 0)

    @pl.when(i + 1 < n)
    def _():
        start_tile(i + 1, 1 - slot)

    wait_tile(i, slot)

    gates = gates_ref[...]
    h = h_ref[...] + (gates[:, 0:1] * ybuf[slot, 0] + gates[:, 1:2] * ybuf[slot, 1])
    gate = jax.nn.sigmoid(jnp.dot(h.astype(BF16), wpg_ref[...], preferred_element_type=F32))
    e = jnp.dot(p_ref[...].astype(BF16), wpp_ref[...], preferred_element_type=F32)
    o_ref[...] = h + _rms(e * gate, pn_ref[...])


def moe_combine_ple(dest_flat, h, gates_tk, p, yb, w_ple_gate, w_ple_proj, ple_gain):
    t, d = h.shape
    tm = min(DISPATCH_TILE, t)
    pd = p.shape[1]
    row = lambda i, dest: (i, 0)
    const = lambda i, dest: (0, 0)
    return pl.pallas_call(
        _combine_kernel,
        grid_spec=pltpu.PrefetchScalarGridSpec(
            num_scalar_prefetch=1,
            grid=(t // tm,),
            in_specs=[
                pl.BlockSpec((tm, d), row),
                pl.BlockSpec((tm, TOP_K), row),
                pl.BlockSpec((tm, pd), row),
                pl.BlockSpec(memory_space=pl.ANY),
                pl.BlockSpec((d, d), const),
                pl.BlockSpec((pd, d), const),
                pl.BlockSpec((1, d), const),
            ],
            out_specs=pl.BlockSpec((tm, d), row),
            scratch_shapes=[pltpu.VMEM((2, TOP_K, tm, d), F32), pltpu.SemaphoreType.DMA((2,))],
        ),
        out_shape=jax.ShapeDtypeStruct((t, d), F32),
        compiler_params=_cparams("arbitrary"),
        name="moe_combine_ple",
    )(dest_flat, h, gates_tk, p, yb, w_ple_gate, w_ple_proj, ple_gain.reshape(1, d))


def _moe_ple_layer(h, p_i, norm_ffn, w_group, b_group, w_expert, b_expert, w_gate, w_up, w_down,
                   ple_w_proj, ple_w_gate, ple_norm):
    t, d = h.shape
    ng, _, epg = w_expert.shape
    w_rt = jnp.concatenate([w_group.T, jnp.transpose(w_expert, (0, 2, 1)).reshape(ng * epg, d)], axis=0)
    b_r = jnp.concatenate([b_group, b_expert.reshape(-1)])
    pad = LANES - w_rt.shape[0]
    w_rt = jnp.pad(w_rt, ((0, pad), (0, 0)))
    b_r = jnp.broadcast_to(jnp.pad(b_r, (0, pad))[:, None], (LANES, LANES))

    hn_packed, eidx, gates, rank, counts = moe_router(h, norm_ffn, w_rt, b_r)

    counts = counts[:, 0]
    padded = (counts + MOE_BLOCK - 1) // MOE_BLOCK * MOE_BLOCK
    pad_end = jnp.cumsum(padded)
    pad_start = pad_end - padded
    n_blocks = (t * TOP_K + N_EXPERTS * (MOE_BLOCK - 1) + MOE_BLOCK - 1) // MOE_BLOCK
    dest = (pad_start[eidx] + rank).reshape(-1).astype(jnp.int32)
    block_e = jnp.minimum(
        jnp.searchsorted(pad_end, jnp.arange(n_blocks, dtype=jnp.int32) * MOE_BLOCK, side='right'),
        N_EXPERTS - 1).astype(jnp.int32)

    xin = moe_scatter(dest, hn_packed, n_blocks * MOE_BLOCK)
    yb = moe_experts(block_e, xin, w_gate.astype(BF16), w_up.astype(BF16), w_down.astype(BF16))
    return moe_combine_ple(dest, h, gates.T, p_i, yb, ple_w_gate.astype(BF16), ple_w_proj.astype(BF16), ple_norm)


def _lru_layer(h, batch, norm, w_in, conv_w, conv_b, w_rg, b_rg, w_ig, b_ig, lam, w_out):
    proj = norm_matmul(h, norm, w_in.astype(BF16), tn=1024)
    w_gates = jnp.concatenate([w_rg, w_ig], axis=-1).astype(BF16)
    return lru_core(proj, h, conv_w, conv_b, w_gates, b_rg.reshape(-1), b_ig.reshape(-1), lam,
                    w_out.astype(BF16), batch=batch)


def _attn_layer(h, batch, cos, sin, norm, w_qkv, q_gain, k_gain, w_o):
    ng = q_gain.shape[0]
    head_gains = jnp.stack([q_gain, k_gain, jnp.ones_like(q_gain)], axis=1).reshape(ng * 3, 1, HEAD_DIM)
    qkv = qkv_project(h, norm, w_qkv.astype(BF16), cos, sin, head_gains)
    outs, lses = [], []
    for g, (_, dilation) in enumerate(DILATION_PAIRS):
        o, lse = dilated_attention(qkv, g, dilation, batch=batch)
        outs.append(o)
        lses.append(lse)
    return merge_project(outs, lses, h, w_o.astype(BF16))


def kernel(x, p, positions, norm_mix, norm_ffn, lru_w_in, lru_conv_w, lru_conv_b, lru_w_rg, lru_b_rg, lru_w_ig, lru_b_ig, lru_lambda, lru_w_out, attn_w_qkv, attn_q_gain, attn_k_gain, attn_w_o, moe_w_group, moe_b_group, moe_w_expert, moe_b_expert, moe_w_gate, moe_w_up, moe_w_down, ple_w_proj, ple_w_gate, ple_norm):
    batch, seq, d = x.shape
    depth = p.shape[0]
    t = batch * seq
    h = x.reshape(t, d)
    cos, sin = rope_tables(positions.reshape(t))
    for i in range(depth):
        j = i // 2
        if i % 2 == 0:
            h = _lru_layer(h, batch, norm_mix[i], lru_w_in[j], lru_conv_w[j], lru_conv_b[j], lru_w_rg[j],
                           lru_b_rg[j], lru_w_ig[j], lru_b_ig[j], lru_lambda[j], lru_w_out[j])
        else:
            h = _attn_layer(h, batch, cos, sin, norm_mix[i], attn_w_qkv[j], attn_q_gain[j], attn_k_gain[j],
                            attn_w_o[j])
        h = _moe_ple_layer(h, p[i].reshape(t, -1), norm_ffn[i], moe_w_group[i], moe_b_group[i],
                           moe_w_expert[i], moe_b_expert[i], moe_w_gate[i], moe_w_up[i], moe_w_down[i],
                           ple_w_proj[i], ple_w_gate[i], ple_norm[i])
    return h.reshape(batch, seq, d)
```

```python
import functools
import math

import jax
import jax.numpy as jnp
from jax import lax
from jax.experimental import pallas as pl
from jax.experimental.pallas import tpu as pltpu

NORM_EPS = 1e-6
N_LRU_BLOCKS = 8
CONV_WIDTH = 4
LRU_C = 8.0
DILATION_PAIRS = ((128, 1), (512, 4), (2048, 16))
N_HEADS = 8
HEAD_DIM = 128
ATTN_BLOCK = 128
ROPE_THETA = 10000.0
NEG_INF = -1e30
N_GROUPS = 8
EXPERTS_PER_GROUP = 8
N_EXPERTS = N_GROUPS * EXPERTS_PER_GROUP
TOP_K = 2

LANES = 128
SUBLANES = 8
VMEM_LIMIT_BYTES = 56 * 1024 * 1024

ROW_TILE = 512
LRU_CHUNK = 256
ATTN_TILE = 512
MOE_BLOCK = 256
DISPATCH_TILE = 256
QKV_TILE = 256
QKV_CHUNK = 256
MERGE_TILE = 256
DMA_UNROLL = 8

F32 = jnp.float32
BF16 = jnp.bfloat16


def _cparams(*sem):
    return pltpu.CompilerParams(dimension_semantics=sem, vmem_limit_bytes=VMEM_LIMIT_BYTES)


def _rms(x, gain):
    ms = jnp.mean(x * x, axis=-1, keepdims=True)
    return x * lax.rsqrt(ms + NORM_EPS) * gain


def _norm_matmul_kernel(x_ref, g_ref, w_ref, o_ref, xn_ref):
    @pl.when(pl.program_id(1) == 0)
    def _():
        xn_ref[...] = _rms(x_ref[...], g_ref[...]).astype(BF16)

    o_ref[...] = jnp.dot(xn_ref[...], w_ref[...], preferred_element_type=F32).astype(o_ref.dtype)


def norm_matmul(x, gain, w, *, tn, out_dtype=F32):
    t, k = x.shape
    n = w.shape[1]
    tm = min(ROW_TILE, t)
    return pl.pallas_call(
        _norm_matmul_kernel,
        grid=(t // tm, n // tn),
        in_specs=[
            pl.BlockSpec((tm, k), lambda i, j: (i, 0)),
            pl.BlockSpec((1, k), lambda i, j: (0, 0)),
            pl.BlockSpec((k, tn), lambda i, j: (0, j)),
        ],
        out_specs=pl.BlockSpec((tm, tn), lambda i, j: (i, j)),
        out_shape=jax.ShapeDtypeStruct((t, n), out_dtype),
        scratch_shapes=[pltpu.VMEM((tm, k), BF16)],
        compiler_params=_cparams("parallel", "arbitrary"),
        name="norm_matmul",
    )(x, gain.reshape(1, k), w)


def _shift_rows_in_vreg(x, s, fill):
    g = x.shape[0] // SUBLANES
    x3 = x.reshape(g, SUBLANES, x.shape[1])
    rolled = pltpu.roll(x3, s, axis=1)
    row = lax.broadcasted_iota(jnp.int32, x3.shape, 1)
    return jnp.where(row >= s, rolled, fill).reshape(x.shape)


def _shift_rows(x, s, fill):
    row = lax.broadcasted_iota(jnp.int32, x.shape, 0)
    return jnp.where(row >= s, pltpu.roll(x, s, axis=0), fill)


def _one_minus_exp(x, ex):
    near = jnp.where(ex == 1.0, -x, (1.0 - ex) * x / jnp.log(ex))
    return jnp.where(x > -0.5, near, 1.0 - ex)


def _lru_kernel(gate_ref, xr_ref, hres_ref, cw_ref, cb_ref, wg_ref, brg_ref, big_ref, lam_ref, wo_ref,
                o_ref, xprev_ref, hcarry_ref, xc_ref, a_ref, u_ref, hin_ref, tot_ref):
    tc, d = xr_ref.shape
    blk = d // N_LRU_BLOCKS

    @pl.when(pl.program_id(1) == 0)
    def _():
        xprev_ref[...] = jnp.zeros_like(xprev_ref)
        hcarry_ref[...] = jnp.zeros_like(hcarry_ref)

    x = xr_ref[...]
    prev = xprev_ref[...]
    cw = cw_ref[...]
    acc = cb_ref[...] + cw[CONV_WIDTH - 1:CONV_WIDTH] * x
    head = cb_ref[...] + cw[CONV_WIDTH - 1:CONV_WIDTH] * x[0:SUBLANES]
    row8 = lax.broadcasted_iota(jnp.int32, (SUBLANES, d), 0)
    for s in range(1, CONV_WIDTH):
        wk = cw[CONV_WIDTH - 1 - s:CONV_WIDTH - s]
        acc = acc + wk * pltpu.roll(x, s, axis=0)
        first = jnp.where(row8 >= s, pltpu.roll(x[0:SUBLANES], s, axis=0), pltpu.roll(prev, s, axis=0))
        head = head + wk * first
    xc_ref[...] = acc
    xc_ref[0:SUBLANES, :] = head
    xprev_ref[...] = x[tc - SUBLANES:tc]

    lam = lam_ref[...]
    log_sig = jnp.minimum(lam, 0.0) - jnp.log1p(jnp.exp(-jnp.abs(lam)))

    for n in range(N_LRU_BLOCKS):
        sl = slice(n * blk, (n + 1) * blk)
        xc = xc_ref[:, sl]
        z = jnp.dot(xc.astype(BF16), wg_ref[n], preferred_element_type=F32)
        r = jax.nn.sigmoid(z[:, :blk] + brg_ref[:, sl])
        i = jax.nn.sigmoid(z[:, blk:] + big_ref[:, sl])
        log_a = LRU_C * r * log_sig[:, sl]
        a = jnp.exp(log_a)
        a_ref[:, sl] = a
        u_ref[:, sl] = jnp.sqrt(_one_minus_exp(2.0 * log_a, a * a)) * (i * xc)

    a = a_ref[...]
    u = u_ref[...]
    for s in (1, 2, 4):
        a_sh = _shift_rows_in_vreg(a, s, 1.0)
        u_sh = _shift_rows_in_vreg(u, s, 0.0)
        u = a * u_sh + u
        a = a * a_sh
    a_ref[...] = a
    u_ref[...] = u
    ng = tc // SUBLANES
    for j in range(ng):
        last = slice(j * SUBLANES + SUBLANES - 1, (j + 1) * SUBLANES)
        hin_ref[j:j + 1, :] = a_ref[last, :]
        tot_ref[j:j + 1, :] = u_ref[last, :]
    pa = hin_ref[...]
    su = tot_ref[...]
    s = 1
    while s < ng:
        su = pa * _shift_rows(su, s, 0.0) + su
        pa = pa * _shift_rows(pa, s, 1.0)
        s *= 2
    h0 = hcarry_ref[...]
    after = pa * h0 + su
    hcarry_ref[...] = after[ng - 1:ng]
    row = lax.broadcasted_iota(jnp.int32, after.shape, 0)
    hin_ref[...] = jnp.where(row == 0, h0, pltpu.roll(after, 1, axis=0))
    for j in range(ng):
        rows = slice(j * SUBLANES, (j + 1) * SUBLANES)
        h_in = jnp.broadcast_to(hin_ref[j:j + 1, :], (SUBLANES, d))
        u_ref[rows, :] = a_ref[rows, :] * h_in + u_ref[rows, :]

    y = jax.nn.gelu(gate_ref[...], approximate=True) * u_ref[...]
    o_ref[...] = hres_ref[...] + jnp.dot(y.astype(BF16), wo_ref[...], preferred_element_type=F32)


def lru_core(proj, h_res, conv_w, conv_b, w_gates, b_rg, b_ig, lam, w_out, *, batch):
    t, d = h_res.shape
    seq = t // batch
    tc = min(LRU_CHUNK, seq)
    nc = seq // tc
    blk = d // N_LRU_BLOCKS
    row = lambda b, c: (b * nc + c, 0)
    const2 = lambda b, c: (0, 0)
    return pl.pallas_call(
        _lru_kernel,
        grid=(batch, nc),
        in_specs=[
            pl.BlockSpec((tc, d), lambda b, c: (b * nc + c, 0)),
            pl.BlockSpec((tc, d), lambda b, c: (b * nc + c, 1)),
            pl.BlockSpec((tc, d), row),
            pl.BlockSpec((CONV_WIDTH, d), const2),
            pl.BlockSpec((1, d), const2),
            pl.BlockSpec((N_LRU_BLOCKS, blk, 2 * blk), lambda b, c: (0, 0, 0)),
            pl.BlockSpec((1, d), const2),
            pl.BlockSpec((1, d), const2),
            pl.BlockSpec((1, d), const2),
            pl.BlockSpec((d, d), const2),
        ],
        out_specs=pl.BlockSpec((tc, d), row),
        out_shape=jax.ShapeDtypeStruct((t, d), F32),
        scratch_shapes=[
            pltpu.VMEM((SUBLANES, d), F32),
            pltpu.VMEM((1, d), F32),
            pltpu.VMEM((tc, d), F32),
            pltpu.VMEM((tc, d), F32),
            pltpu.VMEM((tc, d), F32),
            pltpu.VMEM((tc // SUBLANES, d), F32),
            pltpu.VMEM((tc // SUBLANES, d), F32),
        ],
        compiler_params=_cparams("parallel", "arbitrary"),
        name="lru_core",
    )(proj, proj, h_res, conv_w, conv_b.reshape(1, d), w_gates, b_rg.reshape(1, d), b_ig.reshape(1, d),
      lam.reshape(1, d), w_out)


def _rope_kernel(pos_ref, inv_ref, cos_ref, sin_ref):
    ang = pos_ref[...].astype(F32) * inv_ref[...]
    lane = lax.broadcasted_iota(jnp.int32, ang.shape, 1)
    cos_ref[...] = jnp.cos(ang)
    sin_ref[...] = jnp.where(lane < HEAD_DIM // 2, -jnp.sin(ang), jnp.sin(ang))


def rope_tables(positions_flat):
    t = positions_flat.shape[0]
    tm = min(ROW_TILE, t)
    inv = ROPE_THETA ** (-jnp.arange(0, HEAD_DIM, 2, dtype=F32) / HEAD_DIM)
    inv2 = jnp.concatenate([inv, inv]).reshape(1, HEAD_DIM)
    return pl.pallas_call(
        _rope_kernel,
        grid=(t // tm,),
        in_specs=[pl.BlockSpec((tm, 1), lambda i: (i, 0)), pl.BlockSpec((1, HEAD_DIM), lambda i: (0, 0))],
        out_specs=[pl.BlockSpec((tm, HEAD_DIM), lambda i: (i, 0))] * 2,
        out_shape=[jax.ShapeDtypeStruct((t, HEAD_DIM), F32)] * 2,
        compiler_params=_cparams("parallel"),
        name="rope_tables",
    )(positions_flat.reshape(t, 1), inv2)


def _qkv_kernel(x_ref, g_ref, w_ref, cos_ref, sin_ref, qg_ref, kg_ref, o_ref, stage_ref, *, dilation):
    tm = x_ref.shape[0]
    hd = N_HEADS * HEAD_DIM
    xn = _rms(x_ref[...], g_ref[...]).astype(BF16)
    cos = cos_ref[...]
    sin = sin_ref[...]
    for kind in range(3):
        for c in range(hd // QKV_CHUNK):
            cols = slice(kind * hd + c * QKV_CHUNK, kind * hd + (c + 1) * QKV_CHUNK)
            z = jnp.dot(xn, w_ref[:, cols], preferred_element_type=F32)
            if kind < 2:
                gain = (qg_ref if kind == 0 else kg_ref)[...]
                heads = []
                for hh in range(QKV_CHUNK // HEAD_DIM):
                    y = _rms(z[:, hh * HEAD_DIM:(hh + 1) * HEAD_DIM], gain)
                    y = y * cos + pltpu.roll(y, HEAD_DIM // 2, axis=1) * sin
                    heads.append(y * (HEAD_DIM ** -0.5) if kind == 0 else y)
                z = jnp.concatenate(heads, axis=1)
            if dilation == 1:
                o_ref[0, :, cols] = z.astype(BF16)
            else:
                for hh in range(QKV_CHUNK // LANES):
                    stage_ref[cols.start // LANES + hh] = z[:, hh * LANES:(hh + 1) * LANES]
    if dilation > 1:
        for r in range(dilation):
            for cc in range(3 * hd // LANES):
                o_ref[r, :, cc * LANES:(cc + 1) * LANES] = (
                    stage_ref[cc, pl.ds(r, tm // dilation, stride=dilation), :].astype(BF16))


def qkv_project(x, gain, w_qkv, cos, sin, q_gain, k_gain, group, dilation, *, batch):
    t, k = x.shape
    seq = t // batch
    sub = seq // dilation
    hd3 = 3 * N_HEADS * HEAD_DIM
    tm = min(QKV_TILE, seq)
    per_batch = seq // tm
    row = lambda i: (i, 0)
    const = lambda i: (0, 0)
    return pl.pallas_call(
        functools.partial(_qkv_kernel, dilation=dilation),
        grid=(t // tm,),
        in_specs=[
            pl.BlockSpec((tm, k), row),
            pl.BlockSpec((1, k), const),
            pl.BlockSpec((k, hd3), lambda i: (0, group)),
            pl.BlockSpec((tm, HEAD_DIM), row),
            pl.BlockSpec((tm, HEAD_DIM), row),
            pl.BlockSpec((1, HEAD_DIM), const),
            pl.BlockSpec((1, HEAD_DIM), const),
        ],
        out_specs=pl.BlockSpec((None, dilation, tm // dilation, hd3),
                               lambda i: (i // per_batch, 0, i % per_batch, 0)),
        out_shape=jax.ShapeDtypeStruct((batch, dilation, sub, hd3), BF16),
        scratch_shapes=[pltpu.VMEM((hd3 // LANES, tm, LANES), F32)],
        compiler_params=_cparams("parallel"),
        name=f"qkv_project_d{dilation}",
    )(x, gain.reshape(1, k), w_qkv, cos, sin, q_gain.reshape(1, HEAD_DIM), k_gain.reshape(1, HEAD_DIM))


def _attn_kernel(q_ref, kc_ref, kp_ref, vc_ref, vp_ref, o_ref, lse_ref, kbuf_ref, vbuf_ref):
    tl = q_ref.shape[0]
    blk = ATTN_BLOCK
    first_tile = pl.program_id(2) == 0

    kbuf_ref[0:blk, :] = kp_ref[...]
    kbuf_ref[blk:, :] = kc_ref[...]
    vbuf_ref[0:blk, :] = vp_ref[...]
    vbuf_ref[blk:, :] = vc_ref[...]

    qi = lax.broadcasted_iota(jnp.int32, (blk, 2 * blk), 0)
    kj = lax.broadcasted_iota(jnp.int32, (blk, 2 * blk), 1)
    upper = kj <= qi + blk
    bias_gen = jnp.where(upper & (kj >= qi), 0.0, NEG_INF).astype(F32)
    bias_first = jnp.where(upper & (kj >= blk), 0.0, NEG_INF).astype(F32)
    bias0 = jnp.where(first_tile, bias_first, bias_gen)
    lane = lax.broadcasted_iota(jnp.int32, (blk, LANES), 1)

    for qb in range(tl // blk):
        bias = bias0 if qb == 0 else bias_gen
        rows = slice(qb * blk, (qb + 1) * blk)
        win = slice(qb * blk, (qb + 2) * blk)
        lse_tile = jnp.zeros((blk, LANES), F32)
        for h in range(N_HEADS):
            cols = slice(h * HEAD_DIM, (h + 1) * HEAD_DIM)
            s = lax.dot_general(q_ref[rows, cols], kbuf_ref[win, cols], (((1,), (1,)), ((), ())),
                                preferred_element_type=F32) + bias
            m = jnp.max(s, axis=-1, keepdims=True)
            p = jnp.exp(s - m)
            den = jnp.sum(p, axis=-1, keepdims=True)
            o = jnp.dot(p.astype(BF16), vbuf_ref[win, cols], preferred_element_type=F32)
            o_ref[rows, cols] = o / den
            lse_tile = jnp.where(lane == h, m + jnp.log(den), lse_tile)
        lse_ref[rows, :] = lse_tile


def dilated_attention(qkv_g):
    batch, dilation, sub, hd3 = qkv_g.shape
    hd = hd3 // 3
    tl = min(ATTN_TILE, sub)
    per = tl // ATTN_BLOCK
    cur = lambda kind: (lambda b, r, i: (b, r, i, kind))
    prev = lambda kind: (lambda b, r, i: (b, r, jnp.maximum(i * per - 1, 0), kind))
    return pl.pallas_call(
        _attn_kernel,
        grid=(batch, dilation, sub // tl),
        in_specs=[
            pl.BlockSpec((None, None, tl, hd), cur(0)),
            pl.BlockSpec((None, None, tl, hd), cur(1)),
            pl.BlockSpec((None, None, ATTN_BLOCK, hd), prev(1)),
            pl.BlockSpec((None, None, tl, hd), cur(2)),
            pl.BlockSpec((None, None, ATTN_BLOCK, hd), prev(2)),
        ],
        out_specs=[
            pl.BlockSpec((None, None, tl, hd), lambda b, r, i: (b, r, i, 0)),
            pl.BlockSpec((None, None, tl, LANES), lambda b, r, i: (b, r, i, 0)),
        ],
        out_shape=[
            jax.ShapeDtypeStruct((batch, dilation, sub, hd), F32),
            jax.ShapeDtypeStruct((batch, dilation, sub, LANES), F32),
        ],
        scratch_shapes=[pltpu.VMEM((tl + ATTN_BLOCK, hd), BF16), pltpu.VMEM((tl + ATTN_BLOCK, hd), BF16)],
        compiler_params=_cparams("parallel", "parallel", "arbitrary"),
        name=f"dilated_attention_d{dilation}",
    )(qkv_g, qkv_g, qkv_g, qkv_g, qkv_g)


def _interleave(src_ref, dst_ref):
    dilation, rows, width = src_ref.shape
    for c in range(width // LANES):
        cols = slice(c * LANES, (c + 1) * LANES)
        if dilation == 1:
            dst_ref[c] = src_ref[0, :, cols]
        else:
            for r in range(dilation):
                dst_ref[c, pl.ds(r, rows, stride=dilation), :] = src_ref[r, :, cols]


def _merge_kernel(o0_ref, o1_ref, o2_ref, l0_ref, l1_ref, l2_ref, hres_ref, wo_ref, out_ref, obuf, lbuf):
    for g, (o_ref, l_ref) in enumerate(((o0_ref, l0_ref), (o1_ref, l1_ref), (o2_ref, l2_ref))):
        _interleave(o_ref, obuf.at[g])
        _interleave(l_ref, lbuf.at[g])
    l0, l1, l2 = lbuf[0, 0], lbuf[1, 0], lbuf[2, 0]
    m = jnp.maximum(jnp.maximum(l0, l1), l2)
    e0, e1, e2 = jnp.exp(l0 - m), jnp.exp(l1 - m), jnp.exp(l2 - m)
    tot = e0 + e1 + e2
    w0, w1, w2 = e0 / tot, e1 / tot, e2 / tot
    tm = l0.shape[0]
    parts = []
    for h in range(N_HEADS):
        bc = lambda w: jnp.broadcast_to(w[:, h:h + 1], (tm, HEAD_DIM))
        parts.append(bc(w0) * obuf[0, h] + bc(w1) * obuf[1, h] + bc(w2) * obuf[2, h])
    o = jnp.concatenate(parts, axis=1).astype(BF16)
    out_ref[...] = hres_ref[...] + jnp.dot(o, wo_ref[...], preferred_element_type=F32)


def merge_project(outs, lses, h_res, w_o, *, batch):
    t, d = h_res.shape
    seq = t // batch
    hd = N_HEADS * HEAD_DIM
    tm = min(MERGE_TILE, seq)
    per_batch = seq // tm
    row = lambda i: (i, 0)

    def grouped(a):
        dil, width = a.shape[1], a.shape[3]
        return pl.BlockSpec((None, dil, tm // dil, width), lambda i: (i // per_batch, 0, i % per_batch, 0))

    return pl.pallas_call(
        _merge_kernel,
        grid=(t // tm,),
        in_specs=[grouped(a) for a in outs] + [grouped(a) for a in lses] + [
            pl.BlockSpec((tm, d), row),
            pl.BlockSpec((hd, d), lambda i: (0, 0)),
        ],
        out_specs=pl.BlockSpec((tm, d), row),
        out_shape=jax.ShapeDtypeStruct((t, d), F32),
        scratch_shapes=[pltpu.VMEM((len(outs), hd // LANES, tm, LANES), F32),
                        pltpu.VMEM((len(lses), 1, tm, LANES), F32)],
        compiler_params=_cparams("parallel"),
        name="merge_project",
    )(*outs, *lses, h_res, w_o)


def _router_kernel(h_ref, g_ref, wr_ref, br_ref, hn_ref, eidx_ref, gate_ref, rank_ref, cnt_ref, carry_ref):
    tm = h_ref.shape[0]
    half = h_ref.shape[1] // 2

    @pl.when(pl.program_id(0) == 0)
    def _():
        carry_ref[...] = jnp.zeros_like(carry_ref)

    hn = _rms(h_ref[...], g_ref[...])
    lo = pltpu.bitcast(hn[:, :half].astype(BF16).astype(F32), jnp.uint32)
    hi = pltpu.bitcast(hn[:, half:].astype(BF16).astype(F32), jnp.uint32)
    hn_ref[...] = hi | (lo >> 16)

    logits = lax.dot_general(wr_ref[...], hn, (((1,), (1,)), ((), ())), precision=lax.Precision.HIGHEST,
                             preferred_element_type=F32) + br_ref[:, 0:1]
    lg = logits[0:N_GROUPS]
    gi = lax.broadcasted_iota(jnp.int32, (N_GROUPS, tm), 0)
    m = jnp.max(lg, axis=0, keepdims=True)
    p_top = 1.0 / jnp.sum(jnp.exp(lg - m), axis=0, keepdims=True)
    g_top = jnp.min(jnp.where(lg == m, gi, N_GROUPS), axis=0, keepdims=True)

    le = jnp.zeros((EXPERTS_PER_GROUP, tm), F32)
    for g in range(N_GROUPS):
        lo_row = N_GROUPS + g * EXPERTS_PER_GROUP
        le = jnp.where(g_top == g, logits[lo_row:lo_row + EXPERTS_PER_GROUP], le)
    ei = lax.broadcasted_iota(jnp.int32, (EXPERTS_PER_GROUP, tm), 0)
    v1 = jnp.max(le, axis=0, keepdims=True)
    i1 = jnp.min(jnp.where(le == v1, ei, EXPERTS_PER_GROUP), axis=0, keepdims=True)
    le2 = jnp.where(ei == i1, -jnp.inf, le)
    v2 = jnp.max(le2, axis=0, keepdims=True)
    i2 = jnp.min(jnp.where(le2 == v2, ei, EXPERTS_PER_GROUP), axis=0, keepdims=True)
    e21 = jnp.exp(v2 - v1)
    gate1 = p_top / (1.0 + e21)
    gate2 = p_top * e21 / (1.0 + e21)
    e1 = g_top * EXPERTS_PER_GROUP + i1
    e2 = g_top * EXPERTS_PER_GROUP + i2
    eidx_ref[...] = jnp.concatenate([e1, e2], axis=0)
    gate_ref[...] = jnp.concatenate([gate1, gate2], axis=0)

    ee = lax.broadcasted_iota(jnp.int32, (N_EXPERTS, tm), 0)
    oh1 = ee == e1
    oh2 = ee == e2
    onehot = jnp.where(oh1 | oh2, 1.0, 0.0)
    tri = jnp.where(lax.broadcasted_iota(jnp.int32, (tm, tm), 0) < lax.broadcasted_iota(jnp.int32, (tm, tm), 1),
                    1.0, 0.0).astype(BF16)
    before = jnp.dot(onehot.astype(BF16), tri, preferred_element_type=F32) + carry_ref[:, 0:1]
    r1 = jnp.sum(jnp.where(oh1, before, 0.0), axis=0, keepdims=True)
    r2 = jnp.sum(jnp.where(oh2, before, 0.0), axis=0, keepdims=True)
    rank_ref[...] = jnp.concatenate([r1, r2], axis=0).astype(jnp.int32)
    carry_ref[...] = carry_ref[...] + jnp.sum(onehot, axis=1, keepdims=True)
    cnt_ref[...] = carry_ref[...].astype(jnp.int32)


def moe_router(h, gain, w_router_t, b_router):
    t, d = h.shape
    tm = min(ROW_TILE, t)
    nlog = w_router_t.shape[0]
    return pl.pallas_call(
        _router_kernel,
        grid=(t // tm,),
        in_specs=[
            pl.BlockSpec((tm, d), lambda i: (i, 0)),
            pl.BlockSpec((1, d), lambda i: (0, 0)),
            pl.BlockSpec((nlog, d), lambda i: (0, 0)),
            pl.BlockSpec((nlog, LANES), lambda i: (0, 0)),
        ],
        out_specs=[
            pl.BlockSpec((tm, d // 2), lambda i: (i, 0)),
            pl.BlockSpec((TOP_K, tm), lambda i: (0, i)),
            pl.BlockSpec((TOP_K, tm), lambda i: (0, i)),
            pl.BlockSpec((TOP_K, tm), lambda i: (0, i)),
            pl.BlockSpec((N_EXPERTS, LANES), lambda i: (0, 0)),
        ],
        out_shape=[
            jax.ShapeDtypeStruct((t, d // 2), jnp.uint32),
            jax.ShapeDtypeStruct((TOP_K, t), jnp.int32),
            jax.ShapeDtypeStruct((TOP_K, t), F32),
            jax.ShapeDtypeStruct((TOP_K, t), jnp.int32),
            jax.ShapeDtypeStruct((N_EXPERTS, LANES), jnp.int32),
        ],
        scratch_shapes=[pltpu.VMEM((N_EXPERTS, LANES), F32)],
        compiler_params=_cparams("arbitrary"),
        name="moe_router",
    )(h, gain.reshape(1, d), w_router_t, b_router)


def _scatter_kernel(dest_ref, x_ref, init_ref, o_ref, sem):
    del init_ref
    tm = x_ref.shape[0]
    t = dest_ref.shape[0] // TOP_K
    base = pl.program_id(0) * tm

    def row_copy(r, k):
        d = dest_ref[k * t + base + r]
        return pltpu.make_async_copy(x_ref.at[pl.ds(r, 1)], o_ref.at[pl.ds(d, 1)], sem)

    def start(r, c):
        for k in range(TOP_K):
            row_copy(r, k).start()
        return c

    def wait(r, c):
        for k in range(TOP_K):
            row_copy(r, k).wait()
        return c

    lax.fori_loop(0, tm, start, 0, unroll=DMA_UNROLL)
    lax.fori_loop(0, tm, wait, 0, unroll=DMA_UNROLL)


def moe_scatter(dest_flat, hn_packed, n_slots):
    t, w = hn_packed.shape
    tm = min(DISPATCH_TILE, t)
    init = jnp.zeros((n_slots, w), hn_packed.dtype)
    return pl.pallas_call(
        _scatter_kernel,
        grid_spec=pltpu.PrefetchScalarGridSpec(
            num_scalar_prefetch=1,
            grid=(t // tm,),
            in_specs=[pl.BlockSpec((tm, w), lambda i, dest: (i, 0)), pl.BlockSpec(memory_space=pl.ANY)],
            out_specs=pl.BlockSpec(memory_space=pl.ANY),
            scratch_shapes=[pltpu.SemaphoreType.DMA(())],
        ),
        out_shape=jax.ShapeDtypeStruct((n_slots, w), hn_packed.dtype),
        input_output_aliases={2: 0},
        compiler_params=_cparams("arbitrary"),
        name="moe_scatter",
    )(dest_flat, hn_packed, init)


def _expert_kernel(be_ref, x_ref, wg_ref, wu_ref, wd_ref, o_ref, wg_bf, wu_bf, wd_bf):
    i = pl.program_id(0)
    new_expert = jnp.logical_or(i == 0, be_ref[i] != be_ref[jnp.maximum(i - 1, 0)])

    @pl.when(new_expert)
    def _():
        wg_bf[...] = wg_ref[...].astype(BF16)
        wu_bf[...] = wu_ref[...].astype(BF16)
        wd_bf[...] = wd_ref[...].astype(BF16)

    xw = x_ref[...]
    lo = pltpu.bitcast(xw << 16, F32).astype(BF16)
    hi = pltpu.bitcast(xw & jnp.uint32(0xFFFF0000), F32).astype(BF16)
    x = jnp.concatenate([lo, hi], axis=1)
    g = jnp.dot(x, wg_bf[...], preferred_element_type=F32)
    u = jnp.dot(x, wu_bf[...], preferred_element_type=F32)
    a = (g * jax.nn.sigmoid(g) * u).astype(BF16)
    o_ref[...] = jnp.dot(a, wd_bf[...], preferred_element_type=F32)


def moe_experts(block_e, xin, w_gate, w_up, w_down, layer):
    p, w = xin.shape
    d = 2 * w
    de = w_gate.shape[3]
    return pl.pallas_call(
        _expert_kernel,
        grid_spec=pltpu.PrefetchScalarGridSpec(
            num_scalar_prefetch=1,
            grid=(p // MOE_BLOCK,),
            in_specs=[
                pl.BlockSpec((MOE_BLOCK, w), lambda i, be: (i, 0)),
                pl.BlockSpec((None, None, d, de), lambda i, be: (layer, be[i], 0, 0)),
                pl.BlockSpec((None, None, d, de), lambda i, be: (layer, be[i], 0, 0)),
                pl.BlockSpec((None, None, de, d), lambda i, be: (layer, be[i], 0, 0)),
            ],
            out_specs=pl.BlockSpec((MOE_BLOCK, d), lambda i, be: (i, 0)),
            scratch_shapes=[pltpu.VMEM((d, de), BF16), pltpu.VMEM((d, de), BF16), pltpu.VMEM((de, d), BF16)],
        ),
        out_shape=jax.ShapeDtypeStruct((p, d), F32),
        compiler_params=_cparams("arbitrary"),
        name="moe_experts",
    )(block_e, xin, w_gate, w_up, w_down)


def _combine_kernel(dest_ref, h_ref, gates_ref, p_ref, y_hbm, wpg_ref, wpp_ref, pn_ref, o_ref, ybuf, sems):
    tm = h_ref.shape[0]
    t = dest_ref.shape[0] // TOP_K
    i = pl.program_id(0)
    n = pl.num_programs(0)

    def row_copy(tile, slot, r, k):
        d = dest_ref[k * t + tile * tm + r]
        return pltpu.make_async_copy(y_hbm.at[pl.ds(d, 1)], ybuf.at[slot, k, pl.ds(r, 1)], sems.at[slot])

    def start_tile(tile, slot):
        def body(r, c):
            for k in range(TOP_K):
                row_copy(tile, slot, r, k).start()
            return c
        lax.fori_loop(0, tm, body, 0, unroll=DMA_UNROLL)

    def wait_tile(tile, slot):
        def body(r, c):
            for k in range(TOP_K):
                row_copy(tile, slot, r, k).wait()
            return c
        lax.fori_loop(0, tm, body, 0, unroll=DMA_UNROLL)

    slot = i % 2

    @pl.when(i == 0)
    def _():
        start_tile(0, 0)

    @pl.when(i + 1 < n)
    def _():
        start_tile(i + 1, 1 - slot)

    wait_tile(i, slot)

    gates = gates_ref[...]
    h = h_ref[...] + (gates[:, 0:1] * ybuf[slot, 0] + gates[:, 1:2] * ybuf[slot, 1])
    gate = jax.nn.sigmoid(jnp.dot(h.astype(BF16), wpg_ref[...], preferred_element_type=F32))
    e = jnp.dot(p_ref[...].astype(BF16), wpp_ref[...], preferred_element_type=F32)
    o_ref[...] = h + _rms(e * gate, pn_ref[...])


def moe_combine_ple(dest_flat, h, gates_tk, p, yb, w_ple_gate, w_ple_proj, ple_gain):
    t, d = h.shape
    tm = min(DISPATCH_TILE, t)
    pd = p.shape[1]
    row = lambda i, dest: (i, 0)
    const = lambda i, dest: (0, 0)
    return pl.pallas_call(
        _combine_kernel,
        grid_spec=pltpu.PrefetchScalarGridSpec(
            num_scalar_prefetch=1,
            grid=(t // tm,),
            in_specs=[
                pl.BlockSpec((tm, d), row),
                pl.BlockSpec((tm, TOP_K), row),
                pl.BlockSpec((tm, pd), row),
                pl.BlockSpec(memory_space=pl.ANY),
                pl.BlockSpec((d, d), const),
                pl.BlockSpec((pd, d), const),
                pl.BlockSpec((1, d), const),
            ],
            out_specs=pl.BlockSpec((tm, d), row),
            scratch_shapes=[pltpu.VMEM((2, TOP_K, tm, d), F32), pltpu.SemaphoreType.DMA((2,))],
        ),
        out_shape=jax.ShapeDtypeStruct((t, d), F32),
        compiler_params=_cparams("arbitrary"),
        name="moe_combine_ple",
    )(dest_flat, h, gates_tk, p, yb, w_ple_gate, w_ple_proj, ple_gain.reshape(1, d))


def _moe_ple_layer(h, p_i, norm_ffn, w_group, b_group, w_expert, b_expert, w_gate, w_up, w_down, layer,
                   ple_w_proj, ple_w_gate, ple_norm):
    t, d = h.shape
    ng, _, epg = w_expert.shape
    w_rt = jnp.concatenate([w_group.T, jnp.transpose(w_expert, (0, 2, 1)).reshape(ng * epg, d)], axis=0)
    b_r = jnp.concatenate([b_group, b_expert.reshape(-1)])
    pad = LANES - w_rt.shape[0]
    w_rt = jnp.pad(w_rt, ((0, pad), (0, 0)))
    b_r = jnp.broadcast_to(jnp.pad(b_r, (0, pad))[:, None], (LANES, LANES))

    hn_packed, eidx, gates, rank, counts = moe_router(h, norm_ffn, w_rt, b_r)

    counts = counts[:, 0]
    padded = (counts + MOE_BLOCK - 1) // MOE_BLOCK * MOE_BLOCK
    pad_end = jnp.cumsum(padded)
    pad_start = pad_end - padded
    n_blocks = (t * TOP_K + N_EXPERTS * (MOE_BLOCK - 1) + MOE_BLOCK - 1) // MOE_BLOCK
    expert_ids = jnp.arange(N_EXPERTS, dtype=jnp.int32)
    run_start = jnp.sum(jnp.where(eidx[..., None] == expert_ids, pad_start, 0), axis=-1)
    dest = (run_start + rank).reshape(-1).astype(jnp.int32)
    block_start = jnp.arange(n_blocks, dtype=jnp.int32) * MOE_BLOCK
    block_e = jnp.minimum(jnp.sum((pad_end[None, :] <= block_start[:, None]).astype(jnp.int32), axis=1),
                          N_EXPERTS - 1)

    xin = moe_scatter(dest, hn_packed, n_blocks * MOE_BLOCK)
    yb = moe_experts(block_e, xin, w_gate, w_up, w_down, layer)
    return moe_combine_ple(dest, h, gates.T, p_i, yb, ple_w_gate.astype(BF16), ple_w_proj.astype(BF16), ple_norm)


def _lru_layer(h, batch, norm, w_in, conv_w, conv_b, w_rg, b_rg, w_ig, b_ig, lam, w_out):
    proj = norm_matmul(h, norm, w_in.astype(BF16), tn=1024)
    w_gates = jnp.concatenate([w_rg, w_ig], axis=-1).astype(BF16)
    return lru_core(proj, h, conv_w, conv_b, w_gates, b_rg.reshape(-1), b_ig.reshape(-1), lam,
                    w_out.astype(BF16), batch=batch)


def _attn_layer(h, batch, cos, sin, norm, w_qkv, q_gain, k_gain, w_o):
    w_qkv = w_qkv.astype(BF16)
    outs, lses = [], []
    for g, (_, dilation) in enumerate(DILATION_PAIRS):
        qkv_g = qkv_project(h, norm, w_qkv, cos, sin, q_gain[g], k_gain[g], g, dilation, batch=batch)
        o, lse = dilated_attention(qkv_g)
        outs.append(o)
        lses.append(lse)
    return merge_project(outs, lses, h, w_o.astype(BF16), batch=batch)


def kernel(x, p, positions, norm_mix, norm_ffn, lru_w_in, lru_conv_w, lru_conv_b, lru_w_rg, lru_b_rg, lru_w_ig, lru_b_ig, lru_lambda, lru_w_out, attn_w_qkv, attn_q_gain, attn_k_gain, attn_w_o, moe_w_group, moe_b_group, moe_w_expert, moe_b_expert, moe_w_gate, moe_w_up, moe_w_down, ple_w_proj, ple_w_gate, ple_norm):
    batch, seq, d = x.shape
    depth = p.shape[0]
    t = batch * seq
    h = x.reshape(t, d)
    cos, sin = rope_tables(positions.reshape(t))
    for i in range(depth):
        j = i // 2
        if i % 2 == 0:
            h = _lru_layer(h, batch, norm_mix[i], lru_w_in[j], lru_conv_w[j], lru_conv_b[j], lru_w_rg[j],
                           lru_b_rg[j], lru_w_ig[j], lru_b_ig[j], lru_lambda[j], lru_w_out[j])
        else:
            h = _attn_layer(h, batch, cos, sin, norm_mix[i], attn_w_qkv[j], attn_q_gain[j], attn_k_gain[j],
                            attn_w_o[j])
        h = _moe_ple_layer(h, p[i].reshape(t, -1), norm_ffn[i], moe_w_group[i], moe_b_group[i],
                           moe_w_expert[i], moe_b_expert[i], moe_w_gate, moe_w_up, moe_w_down, i,
                           ple_w_proj[i], ple_w_gate[i], ple_norm[i])
    return h.reshape(batch, seq, d)
```

```python
import functools
import math

import jax
import jax.numpy as jnp
from jax import lax
from jax.experimental import pallas as pl
from jax.experimental.pallas import tpu as pltpu

NORM_EPS = 1e-6
N_LRU_BLOCKS = 8
CONV_WIDTH = 4
LRU_C = 8.0
DILATION_PAIRS = ((128, 1), (512, 4), (2048, 16))
N_HEADS = 8
HEAD_DIM = 128
ATTN_BLOCK = 128
ROPE_THETA = 10000.0
NEG_INF = -1e30
N_GROUPS = 8
EXPERTS_PER_GROUP = 8
N_EXPERTS = N_GROUPS * EXPERTS_PER_GROUP
TOP_K = 2

LANES = 128
SUBLANES = 8
VMEM_LIMIT_BYTES = 56 * 1024 * 1024

ROW_TILE = 512
LRU_CHUNK = 256
ATTN_TILE = 512
MOE_BLOCK = 256
DISPATCH_TILE = 256
QKV_TILE = 256
QKV_CHUNK = 256
MERGE_TILE = 256
DMA_UNROLL = 8

F32 = jnp.float32
BF16 = jnp.bfloat16


def _cparams(*sem):
    return pltpu.CompilerParams(dimension_semantics=sem, vmem_limit_bytes=VMEM_LIMIT_BYTES)


def _rms(x, gain):
    ms = jnp.mean(x * x, axis=-1, keepdims=True)
    return x * lax.rsqrt(ms + NORM_EPS) * gain


def _sigmoid(x):
    return 0.5 * jnp.tanh(0.5 * x) + 0.5


def _shift_rows_in_vreg(x, s, fill):
    g = x.shape[0] // SUBLANES
    x3 = x.reshape(g, SUBLANES, x.shape[1])
    rolled = pltpu.roll(x3, s, axis=1)
    row = lax.broadcasted_iota(jnp.int32, x3.shape, 1)
    return jnp.where(row >= s, rolled, fill).reshape(x.shape)


def _shift_rows(x, s, fill):
    row = lax.broadcasted_iota(jnp.int32, x.shape, 0)
    return jnp.where(row >= s, pltpu.roll(x, s, axis=0), fill)


def _one_minus_exp(x, ex):
    near = jnp.where(ex == 1.0, -x, (1.0 - ex) * x / jnp.log(ex))
    return jnp.where(x > -0.5, near, 1.0 - ex)


def _lru_kernel(h_ref, gmix_ref, win_ref, cw_ref, cb_ref, wg_ref, brg_ref, big_ref, lam_ref, wo_ref,
                o_ref, xn_ref, xprev_ref, hcarry_ref, xc_ref, a_ref, u_ref, hin_ref, tot_ref):
    tc, d = h_ref.shape
    blk = d // N_LRU_BLOCKS

    @pl.when(pl.program_id(1) == 0)
    def _():
        xprev_ref[...] = jnp.zeros_like(xprev_ref)
        hcarry_ref[...] = jnp.zeros_like(hcarry_ref)

    xn_ref[...] = _rms(h_ref[...], gmix_ref[...]).astype(BF16)
    x = jnp.dot(xn_ref[...], win_ref[:, d:], preferred_element_type=F32)

    prev = xprev_ref[...]
    cw = cw_ref[...]
    acc = cb_ref[...] + cw[CONV_WIDTH - 1:CONV_WIDTH] * x
    head = cb_ref[...] + cw[CONV_WIDTH - 1:CONV_WIDTH] * x[0:SUBLANES]
    row8 = lax.broadcasted_iota(jnp.int32, (SUBLANES, d), 0)
    for s in range(1, CONV_WIDTH):
        wk = cw[CONV_WIDTH - 1 - s:CONV_WIDTH - s]
        acc = acc + wk * pltpu.roll(x, s, axis=0)
        first = jnp.where(row8 >= s, pltpu.roll(x[0:SUBLANES], s, axis=0), pltpu.roll(prev, s, axis=0))
        head = head + wk * first
    xc_ref[...] = acc
    xc_ref[0:SUBLANES, :] = head
    xprev_ref[...] = x[tc - SUBLANES:tc]

    lam = lam_ref[...]
    log_sig = jnp.minimum(lam, 0.0) - jnp.log1p(jnp.exp(-jnp.abs(lam)))

    for n in range(N_LRU_BLOCKS):
        sl = slice(n * blk, (n + 1) * blk)
        xc = xc_ref[:, sl]
        z = jnp.dot(xc.astype(BF16), wg_ref[n], preferred_element_type=F32)
        r = _sigmoid(z[:, :blk] + brg_ref[:, sl])
        i = _sigmoid(z[:, blk:] + big_ref[:, sl])
        log_a = LRU_C * r * log_sig[:, sl]
        a = jnp.exp(log_a)
        a_ref[:, sl] = a
        u_ref[:, sl] = jnp.sqrt(_one_minus_exp(2.0 * log_a, a * a)) * (i * xc)

    a = a_ref[...]
    u = u_ref[...]
    for s in (1, 2, 4):
        a_sh = _shift_rows_in_vreg(a, s, 1.0)
        u_sh = _shift_rows_in_vreg(u, s, 0.0)
        u = a * u_sh + u
        a = a * a_sh
    a_ref[...] = a
    u_ref[...] = u
    ng = tc // SUBLANES
    for j in range(ng):
        last = slice(j * SUBLANES + SUBLANES - 1, (j + 1) * SUBLANES)
        hin_ref[j:j + 1, :] = a_ref[last, :]
        tot_ref[j:j + 1, :] = u_ref[last, :]
    pa = hin_ref[...]
    su = tot_ref[...]
    s = 1
    while s < ng:
        su = pa * _shift_rows(su, s, 0.0) + su
        pa = pa * _shift_rows(pa, s, 1.0)
        s *= 2
    h0 = hcarry_ref[...]
    after = pa * h0 + su
    hcarry_ref[...] = after[ng - 1:ng]
    row = lax.broadcasted_iota(jnp.int32, after.shape, 0)
    hin_ref[...] = jnp.where(row == 0, h0, pltpu.roll(after, 1, axis=0))
    for j in range(ng):
        rows = slice(j * SUBLANES, (j + 1) * SUBLANES)
        h_in = jnp.broadcast_to(hin_ref[j:j + 1, :], (SUBLANES, d))
        u_ref[rows, :] = a_ref[rows, :] * h_in + u_ref[rows, :]

    gate = jnp.dot(xn_ref[...], win_ref[:, :d], preferred_element_type=F32)
    y = jax.nn.gelu(gate, approximate=True) * u_ref[...]
    o_ref[...] = h_ref[...] + jnp.dot(y.astype(BF16), wo_ref[...], preferred_element_type=F32)


def lru_core(h, gain, w_in, conv_w, conv_b, w_gates, b_rg, b_ig, lam, w_out, *, batch):
    t, d = h.shape
    seq = t // batch
    tc = min(LRU_CHUNK, seq)
    nc = seq // tc
    blk = d // N_LRU_BLOCKS
    row = lambda b, c: (b * nc + c, 0)
    const2 = lambda b, c: (0, 0)
    resident = pl.Buffered(1)
    return pl.pallas_call(
        _lru_kernel,
        grid=(batch, nc),
        in_specs=[
            pl.BlockSpec((tc, d), row),
            pl.BlockSpec((1, d), const2),
            pl.BlockSpec((d, 2 * d), const2, pipeline_mode=resident),
            pl.BlockSpec((CONV_WIDTH, d), const2),
            pl.BlockSpec((1, d), const2),
            pl.BlockSpec((N_LRU_BLOCKS, blk, 2 * blk), lambda b, c: (0, 0, 0), pipeline_mode=resident),
            pl.BlockSpec((1, d), const2),
            pl.BlockSpec((1, d), const2),
            pl.BlockSpec((1, d), const2),
            pl.BlockSpec((d, d), const2, pipeline_mode=resident),
        ],
        out_specs=pl.BlockSpec((tc, d), row),
        out_shape=jax.ShapeDtypeStruct((t, d), F32),
        scratch_shapes=[
            pltpu.VMEM((tc, d), BF16),
            pltpu.VMEM((SUBLANES, d), F32),
            pltpu.VMEM((1, d), F32),
            pltpu.VMEM((tc, d), F32),
            pltpu.VMEM((tc, d), F32),
            pltpu.VMEM((tc, d), F32),
            pltpu.VMEM((tc // SUBLANES, d), F32),
            pltpu.VMEM((tc // SUBLANES, d), F32),
        ],
        compiler_params=_cparams("parallel", "arbitrary"),
        name="lru_core",
    )(h, gain.reshape(1, d), w_in, conv_w, conv_b.reshape(1, d), w_gates, b_rg.reshape(1, d), b_ig.reshape(1, d),
      lam.reshape(1, d), w_out)


def _rope_kernel(pos_ref, inv_ref, cos_ref, sin_ref):
    ang = pos_ref[...].astype(F32) * inv_ref[...]
    lane = lax.broadcasted_iota(jnp.int32, ang.shape, 1)
    cos_ref[...] = jnp.cos(ang)
    sin_ref[...] = jnp.where(lane < HEAD_DIM // 2, -jnp.sin(ang), jnp.sin(ang))


def rope_tables(positions_flat):
    t = positions_flat.shape[0]
    tm = min(ROW_TILE, t)
    inv = ROPE_THETA ** (-jnp.arange(0, HEAD_DIM, 2, dtype=F32) / HEAD_DIM)
    inv2 = jnp.concatenate([inv, inv]).reshape(1, HEAD_DIM)
    return pl.pallas_call(
        _rope_kernel,
        grid=(t // tm,),
        in_specs=[pl.BlockSpec((tm, 1), lambda i: (i, 0)), pl.BlockSpec((1, HEAD_DIM), lambda i: (0, 0))],
        out_specs=[pl.BlockSpec((tm, HEAD_DIM), lambda i: (i, 0))] * 2,
        out_shape=[jax.ShapeDtypeStruct((t, HEAD_DIM), F32)] * 2,
        compiler_params=_cparams("parallel"),
        name="rope_tables",
    )(positions_flat.reshape(t, 1), inv2)


def _qkv_kernel(x_ref, g_ref, w_ref, cos_ref, sin_ref, qg_ref, kg_ref, o_ref, stage_ref, *, dilation):
    tm = x_ref.shape[0]
    hd = N_HEADS * HEAD_DIM
    xn = _rms(x_ref[...], g_ref[...]).astype(BF16)
    cos = cos_ref[...]
    sin = sin_ref[...]
    for kind in range(3):
        for c in range(hd // QKV_CHUNK):
            cols = slice(kind * hd + c * QKV_CHUNK, kind * hd + (c + 1) * QKV_CHUNK)
            z = jnp.dot(xn, w_ref[:, cols], preferred_element_type=F32)
            if kind < 2:
                gain = (qg_ref if kind == 0 else kg_ref)[...]
                heads = []
                for hh in range(QKV_CHUNK // HEAD_DIM):
                    y = _rms(z[:, hh * HEAD_DIM:(hh + 1) * HEAD_DIM], gain)
                    y = y * cos + pltpu.roll(y, HEAD_DIM // 2, axis=1) * sin
                    heads.append(y * (HEAD_DIM ** -0.5) if kind == 0 else y)
                z = jnp.concatenate(heads, axis=1)
            if dilation == 1:
                o_ref[0, :, cols] = z.astype(BF16)
            else:
                for hh in range(QKV_CHUNK // LANES):
                    stage_ref[cols.start // LANES + hh] = z[:, hh * LANES:(hh + 1) * LANES]
    if dilation > 1:
        for r in range(dilation):
            for cc in range(3 * hd // LANES):
                o_ref[r, :, cc * LANES:(cc + 1) * LANES] = (
                    stage_ref[cc, pl.ds(r, tm // dilation, stride=dilation), :].astype(BF16))


def qkv_project(x, gain, w_qkv, cos, sin, q_gain, k_gain, group, dilation, *, batch):
    t, k = x.shape
    seq = t // batch
    sub = seq // dilation
    hd3 = 3 * N_HEADS * HEAD_DIM
    tm = min(QKV_TILE, seq)
    per_batch = seq // tm
    row = lambda i: (i, 0)
    const = lambda i: (0, 0)
    return pl.pallas_call(
        functools.partial(_qkv_kernel, dilation=dilation),
        grid=(t // tm,),
        in_specs=[
            pl.BlockSpec((tm, k), row),
            pl.BlockSpec((1, k), const),
            pl.BlockSpec((k, hd3), lambda i: (0, group)),
            pl.BlockSpec((tm, HEAD_DIM), row),
            pl.BlockSpec((tm, HEAD_DIM), row),
            pl.BlockSpec((1, HEAD_DIM), const),
            pl.BlockSpec((1, HEAD_DIM), const),
        ],
        out_specs=pl.BlockSpec((None, dilation, tm // dilation, hd3),
                               lambda i: (i // per_batch, 0, i % per_batch, 0)),
        out_shape=jax.ShapeDtypeStruct((batch, dilation, sub, hd3), BF16),
        scratch_shapes=[pltpu.VMEM((hd3 // LANES, tm, LANES), F32)],
        compiler_params=_cparams("parallel"),
        name=f"qkv_project_d{dilation}",
    )(x, gain.reshape(1, k), w_qkv, cos, sin, q_gain.reshape(1, HEAD_DIM), k_gain.reshape(1, HEAD_DIM))


def _attn_kernel(q_ref, kc_ref, kp_ref, vc_ref, vp_ref, o_ref, lse_ref, kbuf_ref, vbuf_ref):
    tl = q_ref.shape[0]
    blk = ATTN_BLOCK
    first_tile = pl.program_id(2) == 0

    kbuf_ref[0:blk, :] = kp_ref[...]
    kbuf_ref[blk:, :] = kc_ref[...]
    vbuf_ref[0:blk, :] = vp_ref[...]
    vbuf_ref[blk:, :] = vc_ref[...]

    qi = lax.broadcasted_iota(jnp.int32, (blk, 2 * blk), 0)
    kj = lax.broadcasted_iota(jnp.int32, (blk, 2 * blk), 1)
    upper = kj <= qi + blk
    bias_gen = jnp.where(upper & (kj >= qi), 0.0, NEG_INF).astype(F32)
    bias_first = jnp.where(upper & (kj >= blk), 0.0, NEG_INF).astype(F32)
    bias0 = jnp.where(first_tile, bias_first, bias_gen)
    lane = lax.broadcasted_iota(jnp.int32, (blk, LANES), 1)

    for qb in range(tl // blk):
        bias = bias0 if qb == 0 else bias_gen
        rows = slice(qb * blk, (qb + 1) * blk)
        win = slice(qb * blk, (qb + 2) * blk)
        lse_tile = jnp.zeros((blk, LANES), F32)
        for h in range(N_HEADS):
            cols = slice(h * HEAD_DIM, (h + 1) * HEAD_DIM)
            s = lax.dot_general(q_ref[rows, cols], kbuf_ref[win, cols], (((1,), (1,)), ((), ())),
                                preferred_element_type=F32) + bias
            m = jnp.max(s, axis=-1, keepdims=True)
            p = jnp.exp(s - m)
            den = jnp.sum(p, axis=-1, keepdims=True)
            o = jnp.dot(p.astype(BF16), vbuf_ref[win, cols], preferred_element_type=F32)
            o_ref[rows, cols] = o / den
            lse_tile = jnp.where(lane == h, m + jnp.log(den), lse_tile)
        lse_ref[rows, :] = lse_tile


def dilated_attention(qkv_g):
    batch, dilation, sub, hd3 = qkv_g.shape
    hd = hd3 // 3
    tl = min(ATTN_TILE, sub)
    per = tl // ATTN_BLOCK
    cur = lambda kind: (lambda b, r, i: (b, r, i, kind))
    prev = lambda kind: (lambda b, r, i: (b, r, jnp.maximum(i * per - 1, 0), kind))
    return pl.pallas_call(
        _attn_kernel,
        grid=(batch, dilation, sub // tl),
        in_specs=[
            pl.BlockSpec((None, None, tl, hd), cur(0)),
            pl.BlockSpec((None, None, tl, hd), cur(1)),
            pl.BlockSpec((None, None, ATTN_BLOCK, hd), prev(1)),
            pl.BlockSpec((None, None, tl, hd), cur(2)),
            pl.BlockSpec((None, None, ATTN_BLOCK, hd), prev(2)),
        ],
        out_specs=[
            pl.BlockSpec((None, None, tl, hd), lambda b, r, i: (b, r, i, 0)),
            pl.BlockSpec((None, None, tl, LANES), lambda b, r, i: (b, r, i, 0)),
        ],
        out_shape=[
            jax.ShapeDtypeStruct((batch, dilation, sub, hd), F32),
            jax.ShapeDtypeStruct((batch, dilation, sub, LANES), F32),
        ],
        scratch_shapes=[pltpu.VMEM((tl + ATTN_BLOCK, hd), BF16), pltpu.VMEM((tl + ATTN_BLOCK, hd), BF16)],
        compiler_params=_cparams("parallel", "parallel", "arbitrary"),
        name=f"dilated_attention_d{dilation}",
    )(qkv_g, qkv_g, qkv_g, qkv_g, qkv_g)


def _interleave(src_ref, dst_ref):
    dilation, rows, width = src_ref.shape
    for c in range(width // LANES):
        cols = slice(c * LANES, (c + 1) * LANES)
        if dilation == 1:
            dst_ref[c] = src_ref[0, :, cols]
        else:
            for r in range(dilation):
                dst_ref[c, pl.ds(r, rows, stride=dilation), :] = src_ref[r, :, cols]


def _merge_kernel(o0_ref, o1_ref, o2_ref, l0_ref, l1_ref, l2_ref, hres_ref, wo_ref, out_ref, obuf, lbuf):
    for g, (o_ref, l_ref) in enumerate(((o0_ref, l0_ref), (o1_ref, l1_ref), (o2_ref, l2_ref))):
        _interleave(o_ref, obuf.at[g])
        _interleave(l_ref, lbuf.at[g])
    l0, l1, l2 = lbuf[0, 0], lbuf[1, 0], lbuf[2, 0]
    m = jnp.maximum(jnp.maximum(l0, l1), l2)
    e0, e1, e2 = jnp.exp(l0 - m), jnp.exp(l1 - m), jnp.exp(l2 - m)
    tot = e0 + e1 + e2
    w0, w1, w2 = e0 / tot, e1 / tot, e2 / tot
    tm = l0.shape[0]
    parts = []
    for h in range(N_HEADS):
        bc = lambda w: jnp.broadcast_to(w[:, h:h + 1], (tm, HEAD_DIM))
        parts.append(bc(w0) * obuf[0, h] + bc(w1) * obuf[1, h] + bc(w2) * obuf[2, h])
    o = jnp.concatenate(parts, axis=1).astype(BF16)
    out_ref[...] = hres_ref[...] + jnp.dot(o, wo_ref[...], preferred_element_type=F32)


def merge_project(outs, lses, h_res, w_o, *, batch):
    t, d = h_res.shape
    seq = t // batch
    hd = N_HEADS * HEAD_DIM
    tm = min(MERGE_TILE, seq)
    per_batch = seq // tm
    row = lambda i: (i, 0)

    def grouped(a):
        dil, width = a.shape[1], a.shape[3]
        return pl.BlockSpec((None, dil, tm // dil, width), lambda i: (i // per_batch, 0, i % per_batch, 0))

    return pl.pallas_call(
        _merge_kernel,
        grid=(t // tm,),
        in_specs=[grouped(a) for a in outs] + [grouped(a) for a in lses] + [
            pl.BlockSpec((tm, d), row),
            pl.BlockSpec((hd, d), lambda i: (0, 0)),
        ],
        out_specs=pl.BlockSpec((tm, d), row),
        out_shape=jax.ShapeDtypeStruct((t, d), F32),
        scratch_shapes=[pltpu.VMEM((len(outs), hd // LANES, tm, LANES), F32),
                        pltpu.VMEM((len(lses), 1, tm, LANES), F32)],
        compiler_params=_cparams("parallel"),
        name="merge_project",
    )(*outs, *lses, h_res, w_o)


def _router_kernel(h_ref, g_ref, wr_ref, br_ref, hn_ref, eidx_ref, gate_ref, rank_ref, cnt_ref, carry_ref):
    tm = h_ref.shape[0]
    half = h_ref.shape[1] // 2

    @pl.when(pl.program_id(0) == 0)
    def _():
        carry_ref[...] = jnp.zeros_like(carry_ref)

    hn = _rms(h_ref[...], g_ref[...])
    lo = pltpu.bitcast(hn[:, :half].astype(BF16).astype(F32), jnp.uint32)
    hi = pltpu.bitcast(hn[:, half:].astype(BF16).astype(F32), jnp.uint32)
    hn_ref[...] = hi | (lo >> 16)

    logits = lax.dot_general(wr_ref[...], hn, (((1,), (1,)), ((), ())), precision=lax.Precision.HIGHEST,
                             preferred_element_type=F32) + br_ref[:, 0:1]
    lg = logits[0:N_GROUPS]
    gi = lax.broadcasted_iota(jnp.int32, (N_GROUPS, tm), 0)
    m = jnp.max(lg, axis=0, keepdims=True)
    p_top = 1.0 / jnp.sum(jnp.exp(lg - m), axis=0, keepdims=True)
    g_top = jnp.min(jnp.where(lg == m, gi, N_GROUPS), axis=0, keepdims=True)

    le = jnp.zeros((EXPERTS_PER_GROUP, tm), F32)
    for g in range(N_GROUPS):
        lo_row = N_GROUPS + g * EXPERTS_PER_GROUP
        le = jnp.where(g_top == g, logits[lo_row:lo_row + EXPERTS_PER_GROUP], le)
    ei = lax.broadcasted_iota(jnp.int32, (EXPERTS_PER_GROUP, tm), 0)
    v1 = jnp.max(le, axis=0, keepdims=True)
    i1 = jnp.min(jnp.where(le == v1, ei, EXPERTS_PER_GROUP), axis=0, keepdims=True)
    le2 = jnp.where(ei == i1, -jnp.inf, le)
    v2 = jnp.max(le2, axis=0, keepdims=True)
    i2 = jnp.min(jnp.where(le2 == v2, ei, EXPERTS_PER_GROUP), axis=0, keepdims=True)
    e21 = jnp.exp(v2 - v1)
    gate1 = p_top / (1.0 + e21)
    gate2 = p_top * e21 / (1.0 + e21)
    e1 = g_top * EXPERTS_PER_GROUP + i1
    e2 = g_top * EXPERTS_PER_GROUP + i2
    eidx_ref[...] = jnp.concatenate([e1, e2], axis=0)
    gate_ref[...] = jnp.concatenate([gate1, gate2], axis=0)

    ee = lax.broadcasted_iota(jnp.int32, (N_EXPERTS, tm), 0)
    oh1 = ee == e1
    oh2 = ee == e2
    onehot = jnp.where(oh1 | oh2, 1.0, 0.0)
    tri = jnp.where(lax.broadcasted_iota(jnp.int32, (tm, tm), 0) < lax.broadcasted_iota(jnp.int32, (tm, tm), 1),
                    1.0, 0.0).astype(BF16)
    before = jnp.dot(onehot.astype(BF16), tri, preferred_element_type=F32) + carry_ref[:, 0:1]
    r1 = jnp.sum(jnp.where(oh1, before, 0.0), axis=0, keepdims=True)
    r2 = jnp.sum(jnp.where(oh2, before, 0.0), axis=0, keepdims=True)
    rank_ref[...] = jnp.concatenate([r1, r2], axis=0).astype(jnp.int32)
    carry_ref[...] = carry_ref[...] + jnp.sum(onehot, axis=1, keepdims=True)
    cnt_ref[...] = carry_ref[...].astype(jnp.int32)


def moe_router(h, gain, w_router_t, b_router):
    t, d = h.shape
    tm = min(ROW_TILE, t)
    nlog = w_router_t.shape[0]
    return pl.pallas_call(
        _router_kernel,
        grid=(t // tm,),
        in_specs=[
            pl.BlockSpec((tm, d), lambda i: (i, 0)),
            pl.BlockSpec((1, d), lambda i: (0, 0)),
            pl.BlockSpec((nlog, d), lambda i: (0, 0)),
            pl.BlockSpec((nlog, LANES), lambda i: (0, 0)),
        ],
        out_specs=[
            pl.BlockSpec((tm, d // 2), lambda i: (i, 0)),
            pl.BlockSpec((TOP_K, tm), lambda i: (0, i)),
            pl.BlockSpec((TOP_K, tm), lambda i: (0, i)),
            pl.BlockSpec((TOP_K, tm), lambda i: (0, i)),
            pl.BlockSpec((N_EXPERTS, LANES), lambda i: (0, 0)),
        ],
        out_shape=[
            jax.ShapeDtypeStruct((t, d // 2), jnp.uint32),
            jax.ShapeDtypeStruct((TOP_K, t), jnp.int32),
            jax.ShapeDtypeStruct((TOP_K, t), F32),
            jax.ShapeDtypeStruct((TOP_K, t), jnp.int32),
            jax.ShapeDtypeStruct((N_EXPERTS, LANES), jnp.int32),
        ],
        scratch_shapes=[pltpu.VMEM((N_EXPERTS, LANES), F32)],
        compiler_params=_cparams("arbitrary"),
        name="moe_router",
    )(h, gain.reshape(1, d), w_router_t, b_router)


def _scatter_kernel(dest_ref, x_ref, init_ref, o_ref, sem):
    del init_ref
    tm = x_ref.shape[0]
    t = dest_ref.shape[0] // TOP_K
    base = pl.program_id(0) * tm

    def row_copy(r, k):
        d = dest_ref[k * t + base + r]
        return pltpu.make_async_copy(x_ref.at[pl.ds(r, 1)], o_ref.at[pl.ds(d, 1)], sem)

    def start(r, c):
        for k in range(TOP_K):
            row_copy(r, k).start()
        return c

    def wait(r, c):
        for k in range(TOP_K):
            row_copy(r, k).wait()
        return c

    lax.fori_loop(0, tm, start, 0, unroll=DMA_UNROLL)
    lax.fori_loop(0, tm, wait, 0, unroll=DMA_UNROLL)


def moe_scatter(dest_flat, hn_packed, n_slots):
    t, w = hn_packed.shape
    tm = min(DISPATCH_TILE, t)
    init = jnp.zeros((n_slots, w), hn_packed.dtype)
    return pl.pallas_call(
        _scatter_kernel,
        grid_spec=pltpu.PrefetchScalarGridSpec(
            num_scalar_prefetch=1,
            grid=(t // tm,),
            in_specs=[pl.BlockSpec((tm, w), lambda i, dest: (i, 0)), pl.BlockSpec(memory_space=pl.ANY)],
            out_specs=pl.BlockSpec(memory_space=pl.ANY),
            scratch_shapes=[pltpu.SemaphoreType.DMA(())],
        ),
        out_shape=jax.ShapeDtypeStruct((n_slots, w), hn_packed.dtype),
        input_output_aliases={2: 0},
        compiler_params=_cparams("arbitrary"),
        name="moe_scatter",
    )(dest_flat, hn_packed, init)


def _expert_kernel(be_ref, first_ref, run_ref, next_ref, used_ref, x_ref, wg_hbm, wu_hbm, wd_hbm, o_ref,
                   wg_f32, wu_f32, wd_f32, wg_bf, wu_bf, wd_bf, sems, *, layer):
    i = pl.program_id(0)
    active = i < used_ref[0]

    def weight_copies(e, slot):
        return (pltpu.make_async_copy(wg_hbm.at[layer, e], wg_f32.at[slot], sems.at[slot, 0]),
                pltpu.make_async_copy(wu_hbm.at[layer, e], wu_f32.at[slot], sems.at[slot, 1]),
                pltpu.make_async_copy(wd_hbm.at[layer, e], wd_f32.at[slot], sems.at[slot, 2]))

    @pl.when(i == 0)
    def _():
        for c in weight_copies(be_ref[0], 0):
            c.start()

    @pl.when(jnp.logical_and(active, first_ref[i] == 1))
    def _():
        slot = run_ref[i] % 2
        nxt = next_ref[i]

        @pl.when(nxt < N_EXPERTS)
        def _():
            for c in weight_copies(nxt, 1 - slot):
                c.start()

        for c in weight_copies(be_ref[i], slot):
            c.wait()
        wg_bf[...] = wg_f32[slot].astype(BF16)
        wu_bf[...] = wu_f32[slot].astype(BF16)
        wd_bf[...] = wd_f32[slot].astype(BF16)

    @pl.when(active)
    def _():
        xw = x_ref[...]
        lo = pltpu.bitcast(xw << 16, F32).astype(BF16)
        hi = pltpu.bitcast(xw & jnp.uint32(0xFFFF0000), F32).astype(BF16)
        x = jnp.concatenate([lo, hi], axis=1)
        g = jnp.dot(x, wg_bf[...], preferred_element_type=F32)
        u = jnp.dot(x, wu_bf[...], preferred_element_type=F32)
        a = (g * _sigmoid(g) * u).astype(BF16)
        o_ref[...] = jnp.dot(a, wd_bf[...], preferred_element_type=F32)

    @pl.when(jnp.logical_not(active))
    def _():
        o_ref[...] = jnp.zeros_like(o_ref)


def moe_experts(block_e, first, run, next_e, n_used, xin, w_gate, w_up, w_down, layer):
    p, w = xin.shape
    d = 2 * w
    de = w_gate.shape[3]
    row = lambda i, *_: (i, 0)
    return pl.pallas_call(
        functools.partial(_expert_kernel, layer=layer),
        grid_spec=pltpu.PrefetchScalarGridSpec(
            num_scalar_prefetch=5,
            grid=(p // MOE_BLOCK,),
            in_specs=[
                pl.BlockSpec((MOE_BLOCK, w), row),
                pl.BlockSpec(memory_space=pl.ANY),
                pl.BlockSpec(memory_space=pl.ANY),
                pl.BlockSpec(memory_space=pl.ANY),
            ],
            out_specs=pl.BlockSpec((MOE_BLOCK, d), row),
            scratch_shapes=[
                pltpu.VMEM((2, d, de), F32), pltpu.VMEM((2, d, de), F32), pltpu.VMEM((2, de, d), F32),
                pltpu.VMEM((d, de), BF16), pltpu.VMEM((d, de), BF16), pltpu.VMEM((de, d), BF16),
                pltpu.SemaphoreType.DMA((2, 3)),
            ],
        ),
        out_shape=jax.ShapeDtypeStruct((p, d), F32),
        compiler_params=_cparams("arbitrary"),
        name="moe_experts",
    )(block_e, first, run, next_e, n_used, xin, w_gate, w_up, w_down)


def _combine_kernel(dest_ref, h_ref, gates_ref, p_ref, y_hbm, wpg_ref, wpp_ref, pn_ref, o_ref, ybuf, sems):
    tm = h_ref.shape[0]
    t = dest_ref.shape[0] // TOP_K
    i = pl.program_id(0)
    n = pl.num_programs(0)

    def row_copy(tile, slot, r, k):
        d = dest_ref[k * t + tile * tm + r]
        return pltpu.make_async_copy(y_hbm.at[pl.ds(d, 1)], ybuf.at[slot, k, pl.ds(r, 1)], sems.at[slot])

    def start_tile(tile, slot):
        def body(r, c):
            for k in range(TOP_K):
                row_copy(tile, slot, r, k).start()
            return c
        lax.fori_loop(0, tm, body, 0, unroll=DMA_UNROLL)

    def wait_tile(tile, slot):
        def body(r, c):
            for k in range(TOP_K):
                row_copy(tile, slot, r, k).wait()
            return c
        lax.fori_loop(0, tm, body, 0, unroll=DMA_UNROLL)

    slot = i % 2

    @pl.when(i == 0)
    def _():
        start_tile(0, 0)

    @pl.when(i + 1 < n)
    def _():
        start_tile(i + 1, 1 - slot)

    wait_tile(i, slot)

    gates = gates_ref[...]
    h = h_ref[...] + (gates[:, 0:1] * ybuf[slot, 0] + gates[:, 1:2] * ybuf[slot, 1])
    gate = _sigmoid(jnp.dot(h.astype(BF16), wpg_ref[...], preferred_element_type=F32))
    e = jnp.dot(p_ref[...].astype(BF16), wpp_ref[...], preferred_element_type=F32)
    o_ref[...] = h + _rms(e * gate, pn_ref[...])


def moe_combine_ple(dest_flat, h, gates_tk, p, yb, w_ple_gate, w_ple_proj, ple_gain):
    t, d = h.shape
    tm = min(DISPATCH_TILE, t)
    pd = p.shape[1]
    row = lambda i, dest: (i, 0)
    const = lambda i, dest: (0, 0)
    return pl.pallas_call(
        _combine_kernel,
        grid_spec=pltpu.PrefetchScalarGridSpec(
            num_scalar_prefetch=1,
            grid=(t // tm,),
            in_specs=[
                pl.BlockSpec((tm, d), row),
                pl.BlockSpec((tm, TOP_K), row),
                pl.BlockSpec((tm, pd), row),
                pl.BlockSpec(memory_space=pl.ANY),
                pl.BlockSpec((d, d), const),
                pl.BlockSpec((pd, d), const),
                pl.BlockSpec((1, d), const),
            ],
            out_specs=pl.BlockSpec((tm, d), row),
            scratch_shapes=[pltpu.VMEM((2, TOP_K, tm, d), F32), pltpu.SemaphoreType.DMA((2,))],
        ),
        out_shape=jax.ShapeDtypeStruct((t, d), F32),
        compiler_params=_cparams("arbitrary"),
        name="moe_combine_ple",
    )(dest_flat, h, gates_tk, p, yb, w_ple_gate, w_ple_proj, ple_gain.reshape(1, d))


def _moe_ple_layer(h, p_i, norm_ffn, w_group, b_group, w_expert, b_expert, w_gate, w_up, w_down, layer,
                   ple_w_proj, ple_w_gate, ple_norm):
    t, d = h.shape
    ng, _, epg = w_expert.shape
    w_rt = jnp.concatenate([w_group.T, jnp.transpose(w_expert, (0, 2, 1)).reshape(ng * epg, d)], axis=0)
    b_r = jnp.concatenate([b_group, b_expert.reshape(-1)])
    pad = LANES - w_rt.shape[0]
    w_rt = jnp.pad(w_rt, ((0, pad), (0, 0)))
    b_r = jnp.broadcast_to(jnp.pad(b_r, (0, pad))[:, None], (LANES, LANES))

    hn_packed, eidx, gates, rank, counts = moe_router(h, norm_ffn, w_rt, b_r)

    counts = counts[:, 0]
    padded = (counts + MOE_BLOCK - 1) // MOE_BLOCK * MOE_BLOCK
    pad_end = jnp.cumsum(padded)
    pad_start = pad_end - padded
    n_blocks = (t * TOP_K + N_EXPERTS * (MOE_BLOCK - 1) + MOE_BLOCK - 1) // MOE_BLOCK
    expert_ids = jnp.arange(N_EXPERTS, dtype=jnp.int32)
    run_start = jnp.sum(jnp.where(eidx[..., None] == expert_ids, pad_start, 0), axis=-1)
    dest = (run_start + rank).reshape(-1).astype(jnp.int32)
    block_start = jnp.arange(n_blocks, dtype=jnp.int32) * MOE_BLOCK
    block_e = jnp.minimum(jnp.sum((pad_end[None, :] <= block_start[:, None]).astype(jnp.int32), axis=1),
                          N_EXPERTS - 1)
    first = jnp.concatenate([jnp.ones((1,), jnp.int32), (block_e[1:] != block_e[:-1]).astype(jnp.int32)])
    run = jnp.cumsum(first) - 1
    later_nonempty = (expert_ids[None, :] > expert_ids[:, None]) & (padded[None, :] > 0)
    next_of = jnp.min(jnp.where(later_nonempty, expert_ids[None, :], N_EXPERTS), axis=1)
    next_e = jnp.sum(jnp.where(block_e[:, None] == expert_ids, next_of, 0), axis=-1)
    n_used = (pad_end[-1:] // MOE_BLOCK).astype(jnp.int32)

    xin = moe_scatter(dest, hn_packed, n_blocks * MOE_BLOCK)
    yb = moe_experts(block_e, first, run.astype(jnp.int32), next_e.astype(jnp.int32), n_used,
                     xin, w_gate, w_up, w_down, layer)
    return moe_combine_ple(dest, h, gates.T, p_i, yb, ple_w_gate.astype(BF16), ple_w_proj.astype(BF16), ple_norm)


def _lru_layer(h, batch, norm, w_in, conv_w, conv_b, w_rg, b_rg, w_ig, b_ig, lam, w_out):
    w_gates = jnp.concatenate([w_rg, w_ig], axis=-1).astype(BF16)
    return lru_core(h, norm, w_in.astype(BF16), conv_w, conv_b, w_gates, b_rg.reshape(-1), b_ig.reshape(-1), lam,
                    w_out.astype(BF16), batch=batch)


def _attn_layer(h, batch, cos, sin, norm, w_qkv, q_gain, k_gain, w_o):
    w_qkv = w_qkv.astype(BF16)
    outs, lses = [], []
    for g, (_, dilation) in enumerate(DILATION_PAIRS):
        qkv_g = qkv_project(h, norm, w_qkv, cos, sin, q_gain[g], k_gain[g], g, dilation, batch=batch)
        o, lse = dilated_attention(qkv_g)
        outs.append(o)
        lses.append(lse)
    return merge_project(outs, lses, h, w_o.astype(BF16), batch=batch)


def kernel(x, p, positions, norm_mix, norm_ffn, lru_w_in, lru_conv_w, lru_conv_b, lru_w_rg, lru_b_rg, lru_w_ig, lru_b_ig, lru_lambda, lru_w_out, attn_w_qkv, attn_q_gain, attn_k_gain, attn_w_o, moe_w_group, moe_b_group, moe_w_expert, moe_b_expert, moe_w_gate, moe_w_up, moe_w_down, ple_w_proj, ple_w_gate, ple_norm):
    batch, seq, d = x.shape
    depth = p.shape[0]
    t = batch * seq
    h = x.reshape(t, d)
    cos, sin = rope_tables(positions.reshape(t))
    for i in range(depth):
        j = i // 2
        if i % 2 == 0:
            h = _lru_layer(h, batch, norm_mix[i], lru_w_in[j], lru_conv_w[j], lru_conv_b[j], lru_w_rg[j],
                           lru_b_rg[j], lru_w_ig[j], lru_b_ig[j], lru_lambda[j], lru_w_out[j])
        else:
            h = _attn_layer(h, batch, cos, sin, norm_mix[i], attn_w_qkv[j], attn_q_gain[j], attn_k_gain[j],
                            attn_w_o[j])
        h = _moe_ple_layer(h, p[i].reshape(t, -1), norm_ffn[i], moe_w_group[i], moe_b_group[i],
                           moe_w_expert[i], moe_b_expert[i], moe_w_gate, moe_w_up, moe_w_down, i,
                           ple_w_proj[i], ple_w_gate[i], ple_norm[i])
    return h.reshape(batch, seq, d)
```

```python
import functools
import math

import jax
import jax.numpy as jnp
from jax import lax
from jax.experimental import pallas as pl
from jax.experimental.pallas import tpu as pltpu

NORM_EPS = 1e-6
N_LRU_BLOCKS = 8
CONV_WIDTH = 4
LRU_C = 8.0
DILATION_PAIRS = ((128, 1), (512, 4), (2048, 16))
N_HEADS = 8
HEAD_DIM = 128
ATTN_BLOCK = 128
ROPE_THETA = 10000.0
NEG_INF = -1e30
N_GROUPS = 8
EXPERTS_PER_GROUP = 8
N_EXPERTS = N_GROUPS * EXPERTS_PER_GROUP
TOP_K = 2

LANES = 128
SUBLANES = 8
VMEM_LIMIT_BYTES = 56 * 1024 * 1024

ROW_TILE = 512
LRU_CHUNK = 256
ATTN_TILE = 512
MOE_BLOCK = 256
DISPATCH_TILE = 256
QKV_TILE = 256
QKV_CHUNK = 256
MERGE_TILE = 256
DMA_UNROLL = 8
COMBINE_SUB = 256
COMBINE_SUBS = 2
COMBINE_AHEAD = 2

F32 = jnp.float32
BF16 = jnp.bfloat16


def _cparams(*sem):
    return pltpu.CompilerParams(dimension_semantics=sem, vmem_limit_bytes=VMEM_LIMIT_BYTES)


def _rms(x, gain):
    ms = jnp.mean(x * x, axis=-1, keepdims=True)
    return x * lax.rsqrt(ms + NORM_EPS) * gain


def _sigmoid(x):
    return 0.5 * jnp.tanh(0.5 * x) + 0.5


def _shift_rows_in_vreg(x, s, fill):
    g = x.shape[0] // SUBLANES
    x3 = x.reshape(g, SUBLANES, x.shape[1])
    rolled = pltpu.roll(x3, s, axis=1)
    row = lax.broadcasted_iota(jnp.int32, x3.shape, 1)
    return jnp.where(row >= s, rolled, fill).reshape(x.shape)


def _shift_rows(x, s, fill):
    row = lax.broadcasted_iota(jnp.int32, x.shape, 0)
    return jnp.where(row >= s, pltpu.roll(x, s, axis=0), fill)


def _one_minus_exp(x, ex):
    near = jnp.where(ex == 1.0, -x, (1.0 - ex) * x / jnp.log(ex))
    return jnp.where(x > -0.5, near, 1.0 - ex)


def _lru_kernel(h_ref, gmix_ref, win_ref, cw_ref, cb_ref, wg_ref, brg_ref, big_ref, lam_ref, wo_ref,
                o_ref, xn_ref, xprev_ref, hcarry_ref, xc_ref, a_ref, u_ref, hin_ref, tot_ref):
    tc, d = h_ref.shape
    blk = d // N_LRU_BLOCKS

    @pl.when(pl.program_id(1) == 0)
    def _():
        xprev_ref[...] = jnp.zeros_like(xprev_ref)
        hcarry_ref[...] = jnp.zeros_like(hcarry_ref)

    xn_ref[...] = _rms(h_ref[...], gmix_ref[...]).astype(BF16)
    x = jnp.dot(xn_ref[...], win_ref[:, d:], preferred_element_type=F32)

    prev = xprev_ref[...]
    cw = cw_ref[...]
    acc = cb_ref[...] + cw[CONV_WIDTH - 1:CONV_WIDTH] * x
    head = cb_ref[...] + cw[CONV_WIDTH - 1:CONV_WIDTH] * x[0:SUBLANES]
    row8 = lax.broadcasted_iota(jnp.int32, (SUBLANES, d), 0)
    for s in range(1, CONV_WIDTH):
        wk = cw[CONV_WIDTH - 1 - s:CONV_WIDTH - s]
        acc = acc + wk * pltpu.roll(x, s, axis=0)
        first = jnp.where(row8 >= s, pltpu.roll(x[0:SUBLANES], s, axis=0), pltpu.roll(prev, s, axis=0))
        head = head + wk * first
    xc_ref[...] = acc
    xc_ref[0:SUBLANES, :] = head
    xprev_ref[...] = x[tc - SUBLANES:tc]

    lam = lam_ref[...]
    log_sig = jnp.minimum(lam, 0.0) - jnp.log1p(jnp.exp(-jnp.abs(lam)))

    for n in range(N_LRU_BLOCKS):
        sl = slice(n * blk, (n + 1) * blk)
        xc = xc_ref[:, sl]
        z = jnp.dot(xc.astype(BF16), wg_ref[n], preferred_element_type=F32)
        r = _sigmoid(z[:, :blk] + brg_ref[:, sl])
        i = _sigmoid(z[:, blk:] + big_ref[:, sl])
        log_a = LRU_C * r * log_sig[:, sl]
        a = jnp.exp(log_a)
        a_ref[:, sl] = a
        u_ref[:, sl] = jnp.sqrt(_one_minus_exp(2.0 * log_a, a * a)) * (i * xc)

    a = a_ref[...]
    u = u_ref[...]
    for s in (1, 2, 4):
        a_sh = _shift_rows_in_vreg(a, s, 1.0)
        u_sh = _shift_rows_in_vreg(u, s, 0.0)
        u = a * u_sh + u
        a = a * a_sh
    a_ref[...] = a
    u_ref[...] = u
    ng = tc // SUBLANES
    for j in range(ng):
        last = slice(j * SUBLANES + SUBLANES - 1, (j + 1) * SUBLANES)
        hin_ref[j:j + 1, :] = a_ref[last, :]
        tot_ref[j:j + 1, :] = u_ref[last, :]
    pa = hin_ref[...]
    su = tot_ref[...]
    s = 1
    while s < ng:
        su = pa * _shift_rows(su, s, 0.0) + su
        pa = pa * _shift_rows(pa, s, 1.0)
        s *= 2
    h0 = hcarry_ref[...]
    after = pa * h0 + su
    hcarry_ref[...] = after[ng - 1:ng]
    row = lax.broadcasted_iota(jnp.int32, after.shape, 0)
    hin_ref[...] = jnp.where(row == 0, h0, pltpu.roll(after, 1, axis=0))
    for j in range(ng):
        rows = slice(j * SUBLANES, (j + 1) * SUBLANES)
        h_in = jnp.broadcast_to(hin_ref[j:j + 1, :], (SUBLANES, d))
        u_ref[rows, :] = a_ref[rows, :] * h_in + u_ref[rows, :]

    gate = jnp.dot(xn_ref[...], win_ref[:, :d], preferred_element_type=F32)
    y = jax.nn.gelu(gate, approximate=True) * u_ref[...]
    o_ref[...] = h_ref[...] + jnp.dot(y.astype(BF16), wo_ref[...], preferred_element_type=F32)


def lru_core(h, gain, w_in, conv_w, conv_b, w_gates, b_rg, b_ig, lam, w_out, *, batch):
    t, d = h.shape
    seq = t // batch
    tc = min(LRU_CHUNK, seq)
    nc = seq // tc
    blk = d // N_LRU_BLOCKS
    row = lambda b, c: (b * nc + c, 0)
    const2 = lambda b, c: (0, 0)
    resident = pl.Buffered(1)
    return pl.pallas_call(
        _lru_kernel,
        grid=(batch, nc),
        in_specs=[
            pl.BlockSpec((tc, d), row),
            pl.BlockSpec((1, d), const2),
            pl.BlockSpec((d, 2 * d), const2, pipeline_mode=resident),
            pl.BlockSpec((CONV_WIDTH, d), const2),
            pl.BlockSpec((1, d), const2),
            pl.BlockSpec((N_LRU_BLOCKS, blk, 2 * blk), lambda b, c: (0, 0, 0), pipeline_mode=resident),
            pl.BlockSpec((1, d), const2),
            pl.BlockSpec((1, d), const2),
            pl.BlockSpec((1, d), const2),
            pl.BlockSpec((d, d), const2, pipeline_mode=resident),
        ],
        out_specs=pl.BlockSpec((tc, d), row),
        out_shape=jax.ShapeDtypeStruct((t, d), F32),
        scratch_shapes=[
            pltpu.VMEM((tc, d), BF16),
            pltpu.VMEM((SUBLANES, d), F32),
            pltpu.VMEM((1, d), F32),
            pltpu.VMEM((tc, d), F32),
            pltpu.VMEM((tc, d), F32),
            pltpu.VMEM((tc, d), F32),
            pltpu.VMEM((tc // SUBLANES, d), F32),
            pltpu.VMEM((tc // SUBLANES, d), F32),
        ],
        compiler_params=_cparams("parallel", "arbitrary"),
        name="lru_core",
    )(h, gain.reshape(1, d), w_in, conv_w, conv_b.reshape(1, d), w_gates, b_rg.reshape(1, d), b_ig.reshape(1, d),
      lam.reshape(1, d), w_out)


def _rope_kernel(pos_ref, inv_ref, cos_ref, sin_ref):
    ang = pos_ref[...].astype(F32) * inv_ref[...]
    lane = lax.broadcasted_iota(jnp.int32, ang.shape, 1)
    cos_ref[...] = jnp.cos(ang)
    sin_ref[...] = jnp.where(lane < HEAD_DIM // 2, -jnp.sin(ang), jnp.sin(ang))


def rope_tables(positions_flat):
    t = positions_flat.shape[0]
    tm = min(ROW_TILE, t)
    inv = ROPE_THETA ** (-jnp.arange(0, HEAD_DIM, 2, dtype=F32) / HEAD_DIM)
    inv2 = jnp.concatenate([inv, inv]).reshape(1, HEAD_DIM)
    return pl.pallas_call(
        _rope_kernel,
        grid=(t // tm,),
        in_specs=[pl.BlockSpec((tm, 1), lambda i: (i, 0)), pl.BlockSpec((1, HEAD_DIM), lambda i: (0, 0))],
        out_specs=[pl.BlockSpec((tm, HEAD_DIM), lambda i: (i, 0))] * 2,
        out_shape=[jax.ShapeDtypeStruct((t, HEAD_DIM), F32)] * 2,
        compiler_params=_cparams("parallel"),
        name="rope_tables",
    )(positions_flat.reshape(t, 1), inv2)


def _qkv_kernel(x_ref, g_ref, w_ref, cos_ref, sin_ref, qg_ref, kg_ref, o_ref, stage_ref, stage2_ref, *, dilation):
    assert dilation in (1, 4, 16)
    tm = x_ref.shape[0]
    hd = N_HEADS * HEAD_DIM
    xn = _rms(x_ref[...], g_ref[...]).astype(BF16)
    cos = cos_ref[...]
    sin = sin_ref[...]
    for kind in range(3):
        for c in range(hd // QKV_CHUNK):
            cols = slice(kind * hd + c * QKV_CHUNK, kind * hd + (c + 1) * QKV_CHUNK)
            z = jnp.dot(xn, w_ref[:, cols], preferred_element_type=F32)
            if kind < 2:
                gain = (qg_ref if kind == 0 else kg_ref)[...]
                heads = []
                for hh in range(QKV_CHUNK // HEAD_DIM):
                    y = _rms(z[:, hh * HEAD_DIM:(hh + 1) * HEAD_DIM], gain)
                    y = y * cos + pltpu.roll(y, HEAD_DIM // 2, axis=1) * sin
                    heads.append(y * (HEAD_DIM ** -0.5) if kind == 0 else y)
                z = jnp.concatenate(heads, axis=1)
            if dilation == 1:
                o_ref[0, :, cols] = z.astype(BF16)
            else:
                for hh in range(QKV_CHUNK // LANES):
                    stage_ref[cols.start // LANES + hh] = z[:, hh * LANES:(hh + 1) * LANES]
    if dilation == 4:
        for r in range(dilation):
            for cc in range(3 * hd // LANES):
                o_ref[r, :, cc * LANES:(cc + 1) * LANES] = (
                    stage_ref[cc, pl.ds(r, tm // dilation, stride=dilation), :].astype(BF16))
    if dilation == 16:
        quarter = tm // 4
        for cc in range(3 * hd // LANES):
            for r1 in range(4):
                stage2_ref[cc, r1 * quarter:(r1 + 1) * quarter, :] = stage_ref[cc, pl.ds(r1, quarter, stride=4), :]
            for r1 in range(4):
                for r2 in range(4):
                    o_ref[r1 + 4 * r2, :, cc * LANES:(cc + 1) * LANES] = (
                        stage2_ref[cc, pl.ds(r1 * quarter + r2, tm // dilation, stride=4), :].astype(BF16))


def qkv_project(x, gain, w_qkv, cos, sin, q_gain, k_gain, group, dilation, *, batch):
    t, k = x.shape
    seq = t // batch
    sub = seq // dilation
    hd3 = 3 * N_HEADS * HEAD_DIM
    tm = min(QKV_TILE, seq)
    per_batch = seq // tm
    row = lambda i: (i, 0)
    const = lambda i: (0, 0)
    return pl.pallas_call(
        functools.partial(_qkv_kernel, dilation=dilation),
        grid=(t // tm,),
        in_specs=[
            pl.BlockSpec((tm, k), row),
            pl.BlockSpec((1, k), const),
            pl.BlockSpec((k, hd3), lambda i: (0, group)),
            pl.BlockSpec((tm, HEAD_DIM), row),
            pl.BlockSpec((tm, HEAD_DIM), row),
            pl.BlockSpec((1, HEAD_DIM), const),
            pl.BlockSpec((1, HEAD_DIM), const),
        ],
        out_specs=pl.BlockSpec((None, dilation, tm // dilation, hd3),
                               lambda i: (i // per_batch, 0, i % per_batch, 0)),
        out_shape=jax.ShapeDtypeStruct((batch, dilation, sub, hd3), BF16),
        scratch_shapes=[pltpu.VMEM((hd3 // LANES, tm if dilation > 1 else SUBLANES, LANES), F32),
                        pltpu.VMEM((hd3 // LANES, tm if dilation == 16 else SUBLANES, LANES), F32)],
        compiler_params=_cparams("parallel"),
        name=f"qkv_project_d{dilation}",
    )(x, gain.reshape(1, k), w_qkv, cos, sin, q_gain.reshape(1, HEAD_DIM), k_gain.reshape(1, HEAD_DIM))


def _attn_kernel(q_ref, kc_ref, kp_ref, vc_ref, vp_ref, o_ref, lse_ref, kbuf_ref, vbuf_ref):
    tl = q_ref.shape[0]
    blk = ATTN_BLOCK
    first_tile = pl.program_id(2) == 0

    kbuf_ref[0:blk, :] = kp_ref[...]
    kbuf_ref[blk:, :] = kc_ref[...]
    vbuf_ref[0:blk, :] = vp_ref[...]
    vbuf_ref[blk:, :] = vc_ref[...]

    qi = lax.broadcasted_iota(jnp.int32, (blk, 2 * blk), 0)
    kj = lax.broadcasted_iota(jnp.int32, (blk, 2 * blk), 1)
    upper = kj <= qi + blk
    bias_gen = jnp.where(upper & (kj >= qi), 0.0, NEG_INF).astype(F32)
    bias_first = jnp.where(upper & (kj >= blk), 0.0, NEG_INF).astype(F32)
    bias0 = jnp.where(first_tile, bias_first, bias_gen)
    lane = lax.broadcasted_iota(jnp.int32, (blk, LANES), 1)

    for qb in range(tl // blk):
        bias = bias0 if qb == 0 else bias_gen
        rows = slice(qb * blk, (qb + 1) * blk)
        win = slice(qb * blk, (qb + 2) * blk)
        lse_tile = jnp.zeros((blk, LANES), F32)
        for h in range(N_HEADS):
            cols = slice(h * HEAD_DIM, (h + 1) * HEAD_DIM)
            s = lax.dot_general(q_ref[rows, cols], kbuf_ref[win, cols], (((1,), (1,)), ((), ())),
                                preferred_element_type=F32) + bias
            m = jnp.max(s, axis=-1, keepdims=True)
            p = jnp.exp(s - m)
            den = jnp.sum(p, axis=-1, keepdims=True)
            o = jnp.dot(p.astype(BF16), vbuf_ref[win, cols], preferred_element_type=F32)
            o_ref[rows, cols] = o / den
            lse_tile = jnp.where(lane == h, m + jnp.log(den), lse_tile)
        lse_ref[rows, :] = lse_tile


def dilated_attention(qkv_g):
    batch, dilation, sub, hd3 = qkv_g.shape
    hd = hd3 // 3
    tl = min(ATTN_TILE, sub)
    per = tl // ATTN_BLOCK
    cur = lambda kind: (lambda b, r, i: (b, r, i, kind))
    prev = lambda kind: (lambda b, r, i: (b, r, jnp.maximum(i * per - 1, 0), kind))
    return pl.pallas_call(
        _attn_kernel,
        grid=(batch, dilation, sub // tl),
        in_specs=[
            pl.BlockSpec((None, None, tl, hd), cur(0)),
            pl.BlockSpec((None, None, tl, hd), cur(1)),
            pl.BlockSpec((None, None, ATTN_BLOCK, hd), prev(1)),
            pl.BlockSpec((None, None, tl, hd), cur(2)),
            pl.BlockSpec((None, None, ATTN_BLOCK, hd), prev(2)),
        ],
        out_specs=[
            pl.BlockSpec((None, None, tl, hd), lambda b, r, i: (b, r, i, 0)),
            pl.BlockSpec((None, None, tl, LANES), lambda b, r, i: (b, r, i, 0)),
        ],
        out_shape=[
            jax.ShapeDtypeStruct((batch, dilation, sub, hd), F32),
            jax.ShapeDtypeStruct((batch, dilation, sub, LANES), F32),
        ],
        scratch_shapes=[pltpu.VMEM((tl + ATTN_BLOCK, hd), BF16), pltpu.VMEM((tl + ATTN_BLOCK, hd), BF16)],
        compiler_params=_cparams("parallel", "parallel", "arbitrary"),
        name=f"dilated_attention_d{dilation}",
    )(qkv_g, qkv_g, qkv_g, qkv_g, qkv_g)


def _interleave(src_ref, dst_ref, tmp_ref):
    dilation, rows, width = src_ref.shape
    assert dilation in (1, 4, 16)
    for c in range(width // LANES):
        cols = slice(c * LANES, (c + 1) * LANES)
        if dilation == 1:
            dst_ref[c] = src_ref[0, :, cols]
        elif dilation == 4:
            for r in range(dilation):
                dst_ref[c, pl.ds(r, rows, stride=dilation), :] = src_ref[r, :, cols]
        else:
            quarter = 4 * rows
            for r1 in range(4):
                for r2 in range(4):
                    tmp_ref[pl.ds(r1 * quarter + r2, rows, stride=4), :] = src_ref[r1 + 4 * r2, :, cols]
            for r1 in range(4):
                dst_ref[c, pl.ds(r1, quarter, stride=4), :] = tmp_ref[r1 * quarter:(r1 + 1) * quarter, :]


def _merge_kernel(o0_ref, o1_ref, o2_ref, l0_ref, l1_ref, l2_ref, hres_ref, wo_ref, out_ref, obuf, lbuf, tmp):
    for g, (o_ref, l_ref) in enumerate(((o0_ref, l0_ref), (o1_ref, l1_ref), (o2_ref, l2_ref))):
        _interleave(o_ref, obuf.at[g], tmp)
        _interleave(l_ref, lbuf.at[g], tmp)
    l0, l1, l2 = lbuf[0, 0], lbuf[1, 0], lbuf[2, 0]
    m = jnp.maximum(jnp.maximum(l0, l1), l2)
    e0, e1, e2 = jnp.exp(l0 - m), jnp.exp(l1 - m), jnp.exp(l2 - m)
    tot = e0 + e1 + e2
    w0, w1, w2 = e0 / tot, e1 / tot, e2 / tot
    tm = l0.shape[0]
    parts = []
    for h in range(N_HEADS):
        bc = lambda w: jnp.broadcast_to(w[:, h:h + 1], (tm, HEAD_DIM))
        parts.append(bc(w0) * obuf[0, h] + bc(w1) * obuf[1, h] + bc(w2) * obuf[2, h])
    o = jnp.concatenate(parts, axis=1).astype(BF16)
    out_ref[...] = hres_ref[...] + jnp.dot(o, wo_ref[...], preferred_element_type=F32)


def merge_project(outs, lses, h_res, w_o, *, batch):
    t, d = h_res.shape
    seq = t // batch
    hd = N_HEADS * HEAD_DIM
    tm = min(MERGE_TILE, seq)
    per_batch = seq // tm
    row = lambda i: (i, 0)

    def grouped(a):
        dil, width = a.shape[1], a.shape[3]
        return pl.BlockSpec((None, dil, tm // dil, width), lambda i: (i // per_batch, 0, i % per_batch, 0))

    return pl.pallas_call(
        _merge_kernel,
        grid=(t // tm,),
        in_specs=[grouped(a) for a in outs] + [grouped(a) for a in lses] + [
            pl.BlockSpec((tm, d), row),
            pl.BlockSpec((hd, d), lambda i: (0, 0)),
        ],
        out_specs=pl.BlockSpec((tm, d), row),
        out_shape=jax.ShapeDtypeStruct((t, d), F32),
        scratch_shapes=[pltpu.VMEM((len(outs), hd // LANES, tm, LANES), F32),
                        pltpu.VMEM((len(lses), 1, tm, LANES), F32),
                        pltpu.VMEM((tm, LANES), F32)],
        compiler_params=_cparams("parallel"),
        name="merge_project",
    )(*outs, *lses, h_res, w_o)


def _router_kernel(h_ref, g_ref, wr_ref, br_ref, hn_ref, eidx_ref, gate_ref, rank_ref, cnt_ref, carry_ref):
    tm = h_ref.shape[0]
    half = h_ref.shape[1] // 2

    @pl.when(pl.program_id(0) == 0)
    def _():
        carry_ref[...] = jnp.zeros_like(carry_ref)

    hn = _rms(h_ref[...], g_ref[...])
    lo = pltpu.bitcast(hn[:, :half].astype(BF16).astype(F32), jnp.uint32)
    hi = pltpu.bitcast(hn[:, half:].astype(BF16).astype(F32), jnp.uint32)
    hn_ref[...] = hi | (lo >> 16)

    logits = lax.dot_general(wr_ref[...], hn, (((1,), (1,)), ((), ())), precision=lax.Precision.HIGHEST,
                             preferred_element_type=F32) + br_ref[:, 0:1]
    lg = logits[0:N_GROUPS]
    gi = lax.broadcasted_iota(jnp.int32, (N_GROUPS, tm), 0)
    m = jnp.max(lg, axis=0, keepdims=True)
    p_top = 1.0 / jnp.sum(jnp.exp(lg - m), axis=0, keepdims=True)
    g_top = jnp.min(jnp.where(lg == m, gi, N_GROUPS), axis=0, keepdims=True)

    le = jnp.zeros((EXPERTS_PER_GROUP, tm), F32)
    for g in range(N_GROUPS):
        lo_row = N_GROUPS + g * EXPERTS_PER_GROUP
        le = jnp.where(g_top == g, logits[lo_row:lo_row + EXPERTS_PER_GROUP], le)
    ei = lax.broadcasted_iota(jnp.int32, (EXPERTS_PER_GROUP, tm), 0)
    v1 = jnp.max(le, axis=0, keepdims=True)
    i1 = jnp.min(jnp.where(le == v1, ei, EXPERTS_PER_GROUP), axis=0, keepdims=True)
    le2 = jnp.where(ei == i1, -jnp.inf, le)
    v2 = jnp.max(le2, axis=0, keepdims=True)
    i2 = jnp.min(jnp.where(le2 == v2, ei, EXPERTS_PER_GROUP), axis=0, keepdims=True)
    e21 = jnp.exp(v2 - v1)
    gate1 = p_top / (1.0 + e21)
    gate2 = p_top * e21 / (1.0 + e21)
    e1 = g_top * EXPERTS_PER_GROUP + i1
    e2 = g_top * EXPERTS_PER_GROUP + i2
    eidx_ref[...] = jnp.concatenate([e1, e2], axis=0)
    gate_ref[...] = jnp.concatenate([gate1, gate2], axis=0)

    ee = lax.broadcasted_iota(jnp.int32, (N_EXPERTS, tm), 0)
    oh1 = ee == e1
    oh2 = ee == e2
    onehot = jnp.where(oh1 | oh2, 1.0, 0.0)
    tri = jnp.where(lax.broadcasted_iota(jnp.int32, (tm, tm), 0) < lax.broadcasted_iota(jnp.int32, (tm, tm), 1),
                    1.0, 0.0).astype(BF16)
    before = jnp.dot(onehot.astype(BF16), tri, preferred_element_type=F32) + carry_ref[:, 0:1]
    r1 = jnp.sum(jnp.where(oh1, before, 0.0), axis=0, keepdims=True)
    r2 = jnp.sum(jnp.where(oh2, before, 0.0), axis=0, keepdims=True)
    rank_ref[...] = jnp.concatenate([r1, r2], axis=0).astype(jnp.int32)
    carry_ref[...] = carry_ref[...] + jnp.sum(onehot, axis=1, keepdims=True)
    cnt_ref[...] = carry_ref[...].astype(jnp.int32)


def moe_router(h, gain, w_router_t, b_router):
    t, d = h.shape
    tm = min(ROW_TILE, t)
    nlog = w_router_t.shape[0]
    return pl.pallas_call(
        _router_kernel,
        grid=(t // tm,),
        in_specs=[
            pl.BlockSpec((tm, d), lambda i: (i, 0)),
            pl.BlockSpec((1, d), lambda i: (0, 0)),
            pl.BlockSpec((nlog, d), lambda i: (0, 0)),
            pl.BlockSpec((nlog, LANES), lambda i: (0, 0)),
        ],
        out_specs=[
            pl.BlockSpec((tm, d // 2), lambda i: (i, 0)),
            pl.BlockSpec((TOP_K, tm), lambda i: (0, i)),
            pl.BlockSpec((TOP_K, tm), lambda i: (0, i)),
            pl.BlockSpec((TOP_K, tm), lambda i: (0, i)),
            pl.BlockSpec((N_EXPERTS, LANES), lambda i: (0, 0)),
        ],
        out_shape=[
            jax.ShapeDtypeStruct((t, d // 2), jnp.uint32),
            jax.ShapeDtypeStruct((TOP_K, t), jnp.int32),
            jax.ShapeDtypeStruct((TOP_K, t), F32),
            jax.ShapeDtypeStruct((TOP_K, t), jnp.int32),
            jax.ShapeDtypeStruct((N_EXPERTS, LANES), jnp.int32),
        ],
        scratch_shapes=[pltpu.VMEM((N_EXPERTS, LANES), F32)],
        compiler_params=_cparams("arbitrary"),
        name="moe_router",
    )(h, gain.reshape(1, d), w_router_t, b_router)


def _scatter_kernel(dest_ref, x_ref, init_ref, o_ref, sem):
    del init_ref
    tm = x_ref.shape[0]
    t = dest_ref.shape[0] // TOP_K
    base = pl.program_id(0) * tm

    def row_copy(r, k):
        d = dest_ref[k * t + base + r]
        return pltpu.make_async_copy(x_ref.at[pl.ds(r, 1)], o_ref.at[pl.ds(d, 1)], sem)

    def wait(r, c):
        for k in range(TOP_K):
            row_copy(r, k).wait()
        return c

    for r in range(tm):
        for k in range(TOP_K):
            row_copy(r, k).start()
    lax.fori_loop(0, tm, wait, 0, unroll=DMA_UNROLL)


def moe_scatter(dest_flat, hn_packed, n_slots):
    t, w = hn_packed.shape
    tm = min(DISPATCH_TILE, t)
    init = jnp.zeros((n_slots, w), hn_packed.dtype)
    return pl.pallas_call(
        _scatter_kernel,
        grid_spec=pltpu.PrefetchScalarGridSpec(
            num_scalar_prefetch=1,
            grid=(t // tm,),
            in_specs=[pl.BlockSpec((tm, w), lambda i, dest: (i, 0)), pl.BlockSpec(memory_space=pl.ANY)],
            out_specs=pl.BlockSpec(memory_space=pl.ANY),
            scratch_shapes=[pltpu.SemaphoreType.DMA(())],
        ),
        out_shape=jax.ShapeDtypeStruct((n_slots, w), hn_packed.dtype),
        input_output_aliases={2: 0},
        compiler_params=_cparams("arbitrary"),
        name="moe_scatter",
    )(dest_flat, hn_packed, init)


def _expert_kernel(be_ref, first_ref, run_ref, next_ref, used_ref, x_ref, wg_hbm, wu_hbm, wd_hbm, o_ref,
                   wg_f32, wu_f32, wd_f32, wg_bf, wu_bf, wd_bf, sems, *, layer):
    i = pl.program_id(0)
    active = i < used_ref[0]

    def weight_copies(e, slot):
        return (pltpu.make_async_copy(wg_hbm.at[layer, e], wg_f32.at[slot], sems.at[slot, 0]),
                pltpu.make_async_copy(wu_hbm.at[layer, e], wu_f32.at[slot], sems.at[slot, 1]),
                pltpu.make_async_copy(wd_hbm.at[layer, e], wd_f32.at[slot], sems.at[slot, 2]))

    @pl.when(i == 0)
    def _():
        for c in weight_copies(be_ref[0], 0):
            c.start()

    @pl.when(jnp.logical_and(active, first_ref[i] == 1))
    def _():
        slot = run_ref[i] % 2
        nxt = next_ref[i]

        @pl.when(nxt < N_EXPERTS)
        def _():
            for c in weight_copies(nxt, 1 - slot):
                c.start()

        for c in weight_copies(be_ref[i], slot):
            c.wait()
        wg_bf[...] = wg_f32[slot].astype(BF16)
        wu_bf[...] = wu_f32[slot].astype(BF16)
        wd_bf[...] = wd_f32[slot].astype(BF16)

    @pl.when(active)
    def _():
        xw = x_ref[...]
        lo = pltpu.bitcast(xw << 16, F32).astype(BF16)
        hi = pltpu.bitcast(xw & jnp.uint32(0xFFFF0000), F32).astype(BF16)
        x = jnp.concatenate([lo, hi], axis=1)
        g = jnp.dot(x, wg_bf[...], preferred_element_type=F32)
        u = jnp.dot(x, wu_bf[...], preferred_element_type=F32)
        a = (g * _sigmoid(g) * u).astype(BF16)
        o_ref[...] = jnp.dot(a, wd_bf[...], preferred_element_type=F32)

    @pl.when(jnp.logical_not(active))
    def _():
        o_ref[...] = jnp.zeros_like(o_ref)


def moe_experts(block_e, first, run, next_e, n_used, xin, w_gate, w_up, w_down, layer):
    p, w = xin.shape
    d = 2 * w
    de = w_gate.shape[3]
    row = lambda i, *_: (i, 0)
    return pl.pallas_call(
        functools.partial(_expert_kernel, layer=layer),
        grid_spec=pltpu.PrefetchScalarGridSpec(
            num_scalar_prefetch=5,
            grid=(p // MOE_BLOCK,),
            in_specs=[
                pl.BlockSpec((MOE_BLOCK, w), row),
                pl.BlockSpec(memory_space=pl.ANY),
                pl.BlockSpec(memory_space=pl.ANY),
                pl.BlockSpec(memory_space=pl.ANY),
            ],
            out_specs=pl.BlockSpec((MOE_BLOCK, d), row),
            scratch_shapes=[
                pltpu.VMEM((2, d, de), F32), pltpu.VMEM((2, d, de), F32), pltpu.VMEM((2, de, d), F32),
                pltpu.VMEM((d, de), BF16), pltpu.VMEM((d, de), BF16), pltpu.VMEM((de, d), BF16),
                pltpu.SemaphoreType.DMA((2, 3)),
            ],
        ),
        out_shape=jax.ShapeDtypeStruct((p, d), F32),
        compiler_params=_cparams("arbitrary"),
        name="moe_experts",
    )(block_e, first, run, next_e, n_used, xin, w_gate, w_up, w_down)


def _combine_kernel(dest_ref, h_ref, gates_ref, p_ref, y_hbm, wpg_ref, wpp_ref, pn_ref, o_ref, ybuf, sems):
    st = h_ref.shape[0] // COMBINE_SUBS
    t = dest_ref.shape[0] // TOP_K
    i = pl.program_id(0)
    n_sub = pl.num_programs(0) * COMBINE_SUBS
    n_slot = ybuf.shape[0]

    def row_copy(sub, slot, r, k):
        d = dest_ref[k * t + sub * st + r]
        return pltpu.make_async_copy(y_hbm.at[pl.ds(d, 1)], ybuf.at[slot, k, pl.ds(r, 1)], sems.at[slot])

    def start_sub_loop(sub, slot):
        def body(r, c):
            for k in range(TOP_K):
                row_copy(sub, slot, r, k).start()
            return c
        lax.fori_loop(0, st, body, 0, unroll=DMA_UNROLL)

    def wait_sub(sub, slot):
        def body(r, c):
            for k in range(TOP_K):
                row_copy(sub, slot, r, k).wait()
            return c
        lax.fori_loop(0, st, body, 0, unroll=DMA_UNROLL)

    @pl.when(i == 0)
    def _():
        for s in range(COMBINE_AHEAD):
            start_sub_loop(s, s)

    for j in range(COMBINE_SUBS):
        sub = i * COMBINE_SUBS + j
        slot = sub % n_slot
        wait_sub(sub, slot)
        rows = slice(j * st, (j + 1) * st)
        gates = gates_ref[rows, :]
        h = h_ref[rows, :] + (gates[:, 0:1] * ybuf[slot, 0] + gates[:, 1:2] * ybuf[slot, 1])
        ahead = jnp.minimum(sub + COMBINE_AHEAD, n_sub - 1)
        ahead_slot = (sub + COMBINE_AHEAD) % n_slot
        for r in range(st):
            for k in range(TOP_K):
                row_copy(ahead, ahead_slot, r, k).start()
        gate = _sigmoid(jnp.dot(h.astype(BF16), wpg_ref[...], preferred_element_type=F32))
        e = jnp.dot(p_ref[rows, :].astype(BF16), wpp_ref[...], preferred_element_type=F32)
        o_ref[rows, :] = h + _rms(e * gate, pn_ref[...])

    @pl.when(i == pl.num_programs(0) - 1)
    def _():
        for s in range(COMBINE_AHEAD):
            wait_sub(n_sub - 1, (n_sub + s) % n_slot)


def moe_combine_ple(dest_flat, h, gates_tk, p, yb, w_ple_gate, w_ple_proj, ple_gain):
    t, d = h.shape
    tm = min(COMBINE_SUB * COMBINE_SUBS, t)
    sub = tm // COMBINE_SUBS
    pd = p.shape[1]
    row = lambda i, dest: (i, 0)
    const = lambda i, dest: (0, 0)
    resident = pl.Buffered(1)
    return pl.pallas_call(
        _combine_kernel,
        grid_spec=pltpu.PrefetchScalarGridSpec(
            num_scalar_prefetch=1,
            grid=(t // tm,),
            in_specs=[
                pl.BlockSpec((tm, d), row),
                pl.BlockSpec((tm, TOP_K), row),
                pl.BlockSpec((tm, pd), row),
                pl.BlockSpec(memory_space=pl.ANY),
                pl.BlockSpec((d, d), const, pipeline_mode=resident),
                pl.BlockSpec((pd, d), const, pipeline_mode=resident),
                pl.BlockSpec((1, d), const),
            ],
            out_specs=pl.BlockSpec((tm, d), row),
            scratch_shapes=[pltpu.VMEM((COMBINE_SUBS + COMBINE_AHEAD, TOP_K, sub, d), F32),
                            pltpu.SemaphoreType.DMA((COMBINE_SUBS + COMBINE_AHEAD,))],
        ),
        out_shape=jax.ShapeDtypeStruct((t, d), F32),
        compiler_params=_cparams("arbitrary"),
        name="moe_combine_ple",
    )(dest_flat, h, gates_tk, p, yb, w_ple_gate, w_ple_proj, ple_gain.reshape(1, d))


def _moe_ple_layer(h, p_i, norm_ffn, w_group, b_group, w_expert, b_expert, w_gate, w_up, w_down, layer,
                   ple_w_proj, ple_w_gate, ple_norm):
    t, d = h.shape
    ng, _, epg = w_expert.shape
    w_rt = jnp.concatenate([w_group.T, jnp.transpose(w_expert, (0, 2, 1)).reshape(ng * epg, d)], axis=0)
    b_r = jnp.concatenate([b_group, b_expert.reshape(-1)])
    pad = LANES - w_rt.shape[0]
    w_rt = jnp.pad(w_rt, ((0, pad), (0, 0)))
    b_r = jnp.broadcast_to(jnp.pad(b_r, (0, pad))[:, None], (LANES, LANES))

    hn_packed, eidx, gates, rank, counts = moe_router(h, norm_ffn, w_rt, b_r)

    counts = counts[:, 0]
    padded = (counts + MOE_BLOCK - 1) // MOE_BLOCK * MOE_BLOCK
    pad_end = jnp.cumsum(padded)
    pad_start = pad_end - padded
    n_blocks = (t * TOP_K + N_EXPERTS * (MOE_BLOCK - 1) + MOE_BLOCK - 1) // MOE_BLOCK
    expert_ids = jnp.arange(N_EXPERTS, dtype=jnp.int32)
    run_start = jnp.sum(jnp.where(eidx[..., None] == expert_ids, pad_start, 0), axis=-1)
    dest = (run_start + rank).reshape(-1).astype(jnp.int32)
    block_start = jnp.arange(n_blocks, dtype=jnp.int32) * MOE_BLOCK
    block_e = jnp.minimum(jnp.sum((pad_end[None, :] <= block_start[:, None]).astype(jnp.int32), axis=1),
                          N_EXPERTS - 1)
    first = jnp.concatenate([jnp.ones((1,), jnp.int32), (block_e[1:] != block_e[:-1]).astype(jnp.int32)])
    run = jnp.cumsum(first) - 1
    later_nonempty = (expert_ids[None, :] > expert_ids[:, None]) & (padded[None, :] > 0)
    next_of = jnp.min(jnp.where(later_nonempty, expert_ids[None, :], N_EXPERTS), axis=1)
    next_e = jnp.sum(jnp.where(block_e[:, None] == expert_ids, next_of, 0), axis=-1)
    n_used = (pad_end[-1:] // MOE_BLOCK).astype(jnp.int32)

    xin = moe_scatter(dest, hn_packed, n_blocks * MOE_BLOCK)
    yb = moe_experts(block_e, first, run.astype(jnp.int32), next_e.astype(jnp.int32), n_used,
                     xin, w_gate, w_up, w_down, layer)
    return moe_combine_ple(dest, h, gates.T, p_i, yb, ple_w_gate.astype(BF16), ple_w_proj.astype(BF16), ple_norm)


def _lru_layer(h, batch, norm, w_in, conv_w, conv_b, w_rg, b_rg, w_ig, b_ig, lam, w_out):
    w_gates = jnp.concatenate([w_rg, w_ig], axis=-1).astype(BF16)
    return lru_core(h, norm, w_in.astype(BF16), conv_w, conv_b, w_gates, b_rg.reshape(-1), b_ig.reshape(-1), lam,
                    w_out.astype(BF16), batch=batch)


def _attn_layer(h, batch, cos, sin, norm, w_qkv, q_gain, k_gain, w_o):
    w_qkv = w_qkv.astype(BF16)
    outs, lses = [], []
    for g, (_, dilation) in enumerate(DILATION_PAIRS):
        qkv_g = qkv_project(h, norm, w_qkv, cos, sin, q_gain[g], k_gain[g], g, dilation, batch=batch)
        o, lse = dilated_attention(qkv_g)
        outs.append(o)
        lses.append(lse)
    return merge_project(outs, lses, h, w_o.astype(BF16), batch=batch)


def kernel(x, p, positions, norm_mix, norm_ffn, lru_w_in, lru_conv_w, lru_conv_b, lru_w_rg, lru_b_rg, lru_w_ig, lru_b_ig, lru_lambda, lru_w_out, attn_w_qkv, attn_q_gain, attn_k_gain, attn_w_o, moe_w_group, moe_b_group, moe_w_expert, moe_b_expert, moe_w_gate, moe_w_up, moe_w_down, ple_w_proj, ple_w_gate, ple_norm):
    batch, seq, d = x.shape
    depth = p.shape[0]
    t = batch * seq
    h = x.reshape(t, d)
    cos, sin = rope_tables(positions.reshape(t))
    for i in range(depth):
        j = i // 2
        if i % 2 == 0:
            h = _lru_layer(h, batch, norm_mix[i], lru_w_in[j], lru_conv_w[j], lru_conv_b[j], lru_w_rg[j],
                           lru_b_rg[j], lru_w_ig[j], lru_b_ig[j], lru_lambda[j], lru_w_out[j])
        else:
            h = _attn_layer(h, batch, cos, sin, norm_mix[i], attn_w_qkv[j], attn_q_gain[j], attn_k_gain[j],
                            attn_w_o[j])
        h = _moe_ple_layer(h, p[i].reshape(t, -1), norm_ffn[i], moe_w_group[i], moe_b_group[i],
                           moe_w_expert[i], moe_b_expert[i], moe_w_gate, moe_w_up, moe_w_down, i,
                           ple_w_proj[i], ple_w_gate[i], ple_norm[i])
    return h.reshape(batch, seq, d)
```

```python
import functools

import jax
import jax.numpy as jnp
from jax import lax
from jax.experimental import pallas as pl
from jax.experimental.pallas import tpu as pltpu

NORM_EPS = 1e-6
N_LRU_BLOCKS = 8
CONV_WIDTH = 4
LRU_C = 8.0
DILATION_PAIRS = ((128, 1), (512, 4), (2048, 16))
N_HEADS = 8
HEAD_DIM = 128
ATTN_BLOCK = 128
ROPE_THETA = 10000.0
NEG_INF = -1e30
N_GROUPS = 8
EXPERTS_PER_GROUP = 8
N_EXPERTS = N_GROUPS * EXPERTS_PER_GROUP
TOP_K = 2

LANES = 128
SUBLANES = 8
VMEM_LIMIT_BYTES = 56 * 1024 * 1024

ROW_TILE = 512
LRU_CHUNK = 256
ATTN_TILE = 512
MOE_BLOCK = 256
EXPERT_AHEAD = 2
QKV_TILE = 256
QKV_CHUNK = 256
MERGE_TILE = 256
DMA_UNROLL = 8
COMBINE_SUB = 256
COMBINE_SUBS = 2
COMBINE_AHEAD = 2

F32 = jnp.float32
BF16 = jnp.bfloat16


def _cparams(*sem):
    return pltpu.CompilerParams(dimension_semantics=sem, vmem_limit_bytes=VMEM_LIMIT_BYTES)


def _rms(x, gain):
    ms = jnp.mean(x * x, axis=-1, keepdims=True)
    return x * lax.rsqrt(ms + NORM_EPS) * gain


def _sigmoid(x):
    return 0.5 * jnp.tanh(0.5 * x) + 0.5


def _shift_rows_in_vreg(x, s, fill):
    g = x.shape[0] // SUBLANES
    x3 = x.reshape(g, SUBLANES, x.shape[1])
    rolled = pltpu.roll(x3, s, axis=1)
    row = lax.broadcasted_iota(jnp.int32, x3.shape, 1)
    return jnp.where(row >= s, rolled, fill).reshape(x.shape)


def _shift_rows(x, s, fill):
    row = lax.broadcasted_iota(jnp.int32, x.shape, 0)
    return jnp.where(row >= s, pltpu.roll(x, s, axis=0), fill)


def _one_minus_exp(x, ex):
    near = jnp.where(ex == 1.0, -x, (1.0 - ex) * x / jnp.log(ex))
    return jnp.where(x > -0.5, near, 1.0 - ex)


def _lru_kernel(h_ref, gmix_ref, win_ref, cw_ref, cb_ref, wg_ref, brg_ref, big_ref, lam_ref, wo_ref,
                o_ref, xn_ref, xprev_ref, hcarry_ref, xc_ref, a_ref, u_ref, hin_ref, tot_ref):
    tc, d = h_ref.shape
    blk = d // N_LRU_BLOCKS

    @pl.when(pl.program_id(1) == 0)
    def _():
        xprev_ref[...] = jnp.zeros_like(xprev_ref)
        hcarry_ref[...] = jnp.zeros_like(hcarry_ref)

    xn_ref[...] = _rms(h_ref[...], gmix_ref[...]).astype(BF16)
    x = jnp.dot(xn_ref[...], win_ref[:, d:], preferred_element_type=F32)

    prev = xprev_ref[...]
    cw = cw_ref[...]
    acc = cb_ref[...] + cw[CONV_WIDTH - 1:CONV_WIDTH] * x
    head = cb_ref[...] + cw[CONV_WIDTH - 1:CONV_WIDTH] * x[0:SUBLANES]
    row8 = lax.broadcasted_iota(jnp.int32, (SUBLANES, d), 0)
    for s in range(1, CONV_WIDTH):
        wk = cw[CONV_WIDTH - 1 - s:CONV_WIDTH - s]
        acc = acc + wk * pltpu.roll(x, s, axis=0)
        first = jnp.where(row8 >= s, pltpu.roll(x[0:SUBLANES], s, axis=0), pltpu.roll(prev, s, axis=0))
        head = head + wk * first
    xc_ref[...] = acc
    xc_ref[0:SUBLANES, :] = head
    xprev_ref[...] = x[tc - SUBLANES:tc]

    lam = lam_ref[...]
    log_sig = jnp.minimum(lam, 0.0) - jnp.log1p(jnp.exp(-jnp.abs(lam)))

    for n in range(N_LRU_BLOCKS):
        sl = slice(n * blk, (n + 1) * blk)
        xc = xc_ref[:, sl]
        z = jnp.dot(xc.astype(BF16), wg_ref[n], preferred_element_type=F32)
        r = _sigmoid(z[:, :blk] + brg_ref[:, sl])
        i = _sigmoid(z[:, blk:] + big_ref[:, sl])
        log_a = LRU_C * r * log_sig[:, sl]
        a = jnp.exp(log_a)
        a_ref[:, sl] = a
        u_ref[:, sl] = jnp.sqrt(_one_minus_exp(2.0 * log_a, a * a)) * (i * xc)

    a = a_ref[...]
    u = u_ref[...]
    for s in (1, 2, 4):
        a_sh = _shift_rows_in_vreg(a, s, 1.0)
        u_sh = _shift_rows_in_vreg(u, s, 0.0)
        u = a * u_sh + u
        a = a * a_sh
    a_ref[...] = a
    u_ref[...] = u
    ng = tc // SUBLANES
    for j in range(ng):
        last = slice(j * SUBLANES + SUBLANES - 1, (j + 1) * SUBLANES)
        hin_ref[j:j + 1, :] = a_ref[last, :]
        tot_ref[j:j + 1, :] = u_ref[last, :]
    pa = hin_ref[...]
    su = tot_ref[...]
    s = 1
    while s < ng:
        su = pa * _shift_rows(su, s, 0.0) + su
        pa = pa * _shift_rows(pa, s, 1.0)
        s *= 2
    h0 = hcarry_ref[...]
    after = pa * h0 + su
    hcarry_ref[...] = after[ng - 1:ng]
    row = lax.broadcasted_iota(jnp.int32, after.shape, 0)
    hin_ref[...] = jnp.where(row == 0, h0, pltpu.roll(after, 1, axis=0))
    for j in range(ng):
        rows = slice(j * SUBLANES, (j + 1) * SUBLANES)
        h_in = jnp.broadcast_to(hin_ref[j:j + 1, :], (SUBLANES, d))
        u_ref[rows, :] = a_ref[rows, :] * h_in + u_ref[rows, :]

    gate = jnp.dot(xn_ref[...], win_ref[:, :d], preferred_element_type=F32)
    y = jax.nn.gelu(gate, approximate=True) * u_ref[...]
    o_ref[...] = h_ref[...] + jnp.dot(y.astype(BF16), wo_ref[...], preferred_element_type=F32)


def lru_core(h, gain, w_in, conv_w, conv_b, w_gates, b_rg, b_ig, lam, w_out, *, batch):
    t, d = h.shape
    seq = t // batch
    tc = min(LRU_CHUNK, seq)
    nc = seq // tc
    blk = d // N_LRU_BLOCKS
    row = lambda b, c: (b * nc + c, 0)
    const2 = lambda b, c: (0, 0)
    resident = pl.Buffered(1)
    return pl.pallas_call(
        _lru_kernel,
        grid=(batch, nc),
        in_specs=[
            pl.BlockSpec((tc, d), row),
            pl.BlockSpec((1, d), const2),
            pl.BlockSpec((d, 2 * d), const2, pipeline_mode=resident),
            pl.BlockSpec((CONV_WIDTH, d), const2),
            pl.BlockSpec((1, d), const2),
            pl.BlockSpec((N_LRU_BLOCKS, blk, 2 * blk), lambda b, c: (0, 0, 0), pipeline_mode=resident),
            pl.BlockSpec((1, d), const2),
            pl.BlockSpec((1, d), const2),
            pl.BlockSpec((1, d), const2),
            pl.BlockSpec((d, d), const2, pipeline_mode=resident),
        ],
        out_specs=pl.BlockSpec((tc, d), row),
        out_shape=jax.ShapeDtypeStruct((t, d), F32),
        scratch_shapes=[
            pltpu.VMEM((tc, d), BF16),
            pltpu.VMEM((SUBLANES, d), F32),
            pltpu.VMEM((1, d), F32),
            pltpu.VMEM((tc, d), F32),
            pltpu.VMEM((tc, d), F32),
            pltpu.VMEM((tc, d), F32),
            pltpu.VMEM((tc // SUBLANES, d), F32),
            pltpu.VMEM((tc // SUBLANES, d), F32),
        ],
        compiler_params=_cparams("parallel", "arbitrary"),
        name="lru_core",
    )(h, gain.reshape(1, d), w_in, conv_w, conv_b.reshape(1, d), w_gates, b_rg.reshape(1, d), b_ig.reshape(1, d),
      lam.reshape(1, d), w_out)


def _rope_kernel(pos_ref, inv_ref, cos_ref, sin_ref):
    ang = pos_ref[...].astype(F32) * inv_ref[...]
    lane = lax.broadcasted_iota(jnp.int32, ang.shape, 1)
    cos_ref[...] = jnp.cos(ang)
    sin_ref[...] = jnp.where(lane < HEAD_DIM // 2, -jnp.sin(ang), jnp.sin(ang))


def rope_tables(positions_flat):
    t = positions_flat.shape[0]
    tm = min(ROW_TILE, t)
    inv = ROPE_THETA ** (-jnp.arange(0, HEAD_DIM, 2, dtype=F32) / HEAD_DIM)
    inv2 = jnp.concatenate([inv, inv]).reshape(1, HEAD_DIM)
    return pl.pallas_call(
        _rope_kernel,
        grid=(t // tm,),
        in_specs=[pl.BlockSpec((tm, 1), lambda i: (i, 0)), pl.BlockSpec((1, HEAD_DIM), lambda i: (0, 0))],
        out_specs=[pl.BlockSpec((tm, HEAD_DIM), lambda i: (i, 0))] * 2,
        out_shape=[jax.ShapeDtypeStruct((t, HEAD_DIM), F32)] * 2,
        compiler_params=_cparams("parallel"),
        name="rope_tables",
    )(positions_flat.reshape(t, 1), inv2)


def _qkv_kernel(x_ref, g_ref, w_ref, cos_ref, sin_ref, qg_ref, kg_ref, o_ref, stage_ref, stage2_ref, *, dilation):
    assert dilation in (1, 4, 16)
    tm = x_ref.shape[0]
    hd = N_HEADS * HEAD_DIM
    xn = _rms(x_ref[...], g_ref[...]).astype(BF16)
    cos = cos_ref[...]
    sin = sin_ref[...]
    for kind in range(3):
        for c in range(hd // QKV_CHUNK):
            cols = slice(kind * hd + c * QKV_CHUNK, kind * hd + (c + 1) * QKV_CHUNK)
            z = jnp.dot(xn, w_ref[:, cols], preferred_element_type=F32)
            if kind < 2:
                gain = (qg_ref if kind == 0 else kg_ref)[...]
                heads = []
                for hh in range(QKV_CHUNK // HEAD_DIM):
                    y = _rms(z[:, hh * HEAD_DIM:(hh + 1) * HEAD_DIM], gain)
                    y = y * cos + pltpu.roll(y, HEAD_DIM // 2, axis=1) * sin
                    heads.append(y * (HEAD_DIM ** -0.5) if kind == 0 else y)
                z = jnp.concatenate(heads, axis=1)
            if dilation == 1:
                o_ref[0, :, cols] = z.astype(BF16)
            else:
                for hh in range(QKV_CHUNK // LANES):
                    stage_ref[cols.start // LANES + hh] = z[:, hh * LANES:(hh + 1) * LANES]
    if dilation == 4:
        for r in range(dilation):
            for cc in range(3 * hd // LANES):
                o_ref[r, :, cc * LANES:(cc + 1) * LANES] = (
                    stage_ref[cc, pl.ds(r, tm // dilation, stride=dilation), :].astype(BF16))
    if dilation == 16:
        quarter = tm // 4
        for cc in range(3 * hd // LANES):
            for r1 in range(4):
                stage2_ref[cc, r1 * quarter:(r1 + 1) * quarter, :] = stage_ref[cc, pl.ds(r1, quarter, stride=4), :]
            for r1 in range(4):
                for r2 in range(4):
                    o_ref[r1 + 4 * r2, :, cc * LANES:(cc + 1) * LANES] = (
                        stage2_ref[cc, pl.ds(r1 * quarter + r2, tm // dilation, stride=4), :].astype(BF16))


def qkv_project(x, gain, w_qkv, cos, sin, q_gain, k_gain, group, dilation, *, batch):
    t, k = x.shape
    seq = t // batch
    sub = seq // dilation
    hd3 = 3 * N_HEADS * HEAD_DIM
    tm = min(QKV_TILE, seq)
    per_batch = seq // tm
    row = lambda i: (i, 0)
    const = lambda i: (0, 0)
    return pl.pallas_call(
        functools.partial(_qkv_kernel, dilation=dilation),
        grid=(t // tm,),
        in_specs=[
            pl.BlockSpec((tm, k), row),
            pl.BlockSpec((1, k), const),
            pl.BlockSpec((k, hd3), lambda i: (0, group)),
            pl.BlockSpec((tm, HEAD_DIM), row),
            pl.BlockSpec((tm, HEAD_DIM), row),
            pl.BlockSpec((1, HEAD_DIM), const),
            pl.BlockSpec((1, HEAD_DIM), const),
        ],
        out_specs=pl.BlockSpec((None, dilation, tm // dilation, hd3),
                               lambda i: (i // per_batch, 0, i % per_batch, 0)),
        out_shape=jax.ShapeDtypeStruct((batch, dilation, sub, hd3), BF16),
        scratch_shapes=[pltpu.VMEM((hd3 // LANES, tm if dilation > 1 else SUBLANES, LANES), F32),
                        pltpu.VMEM((hd3 // LANES, tm if dilation == 16 else SUBLANES, LANES), F32)],
        compiler_params=_cparams("parallel"),
        name=f"qkv_project_d{dilation}",
    )(x, gain.reshape(1, k), w_qkv, cos, sin, q_gain.reshape(1, HEAD_DIM), k_gain.reshape(1, HEAD_DIM))


def _attn_kernel(q_ref, kc_ref, kp_ref, vc_ref, vp_ref, o_ref, lse_ref, kbuf_ref, vbuf_ref):
    tl = q_ref.shape[0]
    blk = ATTN_BLOCK
    first_tile = pl.program_id(2) == 0

    kbuf_ref[0:blk, :] = kp_ref[...]
    kbuf_ref[blk:, :] = kc_ref[...]
    vbuf_ref[0:blk, :] = vp_ref[...]
    vbuf_ref[blk:, :] = vc_ref[...]

    qi = lax.broadcasted_iota(jnp.int32, (blk, 2 * blk), 0)
    kj = lax.broadcasted_iota(jnp.int32, (blk, 2 * blk), 1)
    upper = kj <= qi + blk
    bias_gen = jnp.where(upper & (kj >= qi), 0.0, NEG_INF).astype(F32)
    bias_first = jnp.where(upper & (kj >= blk), 0.0, NEG_INF).astype(F32)
    bias0 = jnp.where(first_tile, bias_first, bias_gen)
    lane = lax.broadcasted_iota(jnp.int32, (blk, LANES), 1)

    for qb in range(tl // blk):
        bias = bias0 if qb == 0 else bias_gen
        rows = slice(qb * blk, (qb + 1) * blk)
        win = slice(qb * blk, (qb + 2) * blk)
        lse_tile = jnp.zeros((blk, LANES), F32)
        for h in range(N_HEADS):
            cols = slice(h * HEAD_DIM, (h + 1) * HEAD_DIM)
            s = lax.dot_general(q_ref[rows, cols], kbuf_ref[win, cols], (((1,), (1,)), ((), ())),
                                preferred_element_type=F32) + bias
            m = jnp.max(s, axis=-1, keepdims=True)
            p = jnp.exp(s - m)
            den = jnp.sum(p, axis=-1, keepdims=True)
            o = jnp.dot(p.astype(BF16), vbuf_ref[win, cols], preferred_element_type=F32)
            o_ref[rows, cols] = o / den
            lse_tile = jnp.where(lane == h, m + jnp.log(den), lse_tile)
        lse_ref[rows, :] = lse_tile


def dilated_attention(qkv_g):
    batch, dilation, sub, hd3 = qkv_g.shape
    hd = hd3 // 3
    tl = min(ATTN_TILE, sub)
    per = tl // ATTN_BLOCK
    cur = lambda kind: (lambda b, r, i: (b, r, i, kind))
    prev = lambda kind: (lambda b, r, i: (b, r, jnp.maximum(i * per - 1, 0), kind))
    return pl.pallas_call(
        _attn_kernel,
        grid=(batch, dilation, sub // tl),
        in_specs=[
            pl.BlockSpec((None, None, tl, hd), cur(0)),
            pl.BlockSpec((None, None, tl, hd), cur(1)),
            pl.BlockSpec((None, None, ATTN_BLOCK, hd), prev(1)),
            pl.BlockSpec((None, None, tl, hd), cur(2)),
            pl.BlockSpec((None, None, ATTN_BLOCK, hd), prev(2)),
        ],
        out_specs=[
            pl.BlockSpec((None, None, tl, hd), lambda b, r, i: (b, r, i, 0)),
            pl.BlockSpec((None, None, tl, LANES), lambda b, r, i: (b, r, i, 0)),
        ],
        out_shape=[
            jax.ShapeDtypeStruct((batch, dilation, sub, hd), F32),
            jax.ShapeDtypeStruct((batch, dilation, sub, LANES), F32),
        ],
        scratch_shapes=[pltpu.VMEM((tl + ATTN_BLOCK, hd), BF16), pltpu.VMEM((tl + ATTN_BLOCK, hd), BF16)],
        compiler_params=_cparams("parallel", "parallel", "arbitrary"),
        name=f"dilated_attention_d{dilation}",
    )(qkv_g, qkv_g, qkv_g, qkv_g, qkv_g)


def _interleave(src_ref, dst_ref, tmp_ref):
    dilation, rows, width = src_ref.shape
    assert dilation in (1, 4, 16)
    for c in range(width // LANES):
        cols = slice(c * LANES, (c + 1) * LANES)
        if dilation == 1:
            dst_ref[c] = src_ref[0, :, cols]
        elif dilation == 4:
            for r in range(dilation):
                dst_ref[c, pl.ds(r, rows, stride=dilation), :] = src_ref[r, :, cols]
        else:
            quarter = 4 * rows
            for r1 in range(4):
                for r2 in range(4):
                    tmp_ref[pl.ds(r1 * quarter + r2, rows, stride=4), :] = src_ref[r1 + 4 * r2, :, cols]
            for r1 in range(4):
                dst_ref[c, pl.ds(r1, quarter, stride=4), :] = tmp_ref[r1 * quarter:(r1 + 1) * quarter, :]


def _merge_kernel(o0_ref, o1_ref, o2_ref, l0_ref, l1_ref, l2_ref, hres_ref, wo_ref, out_ref, obuf, lbuf, tmp):
    for g, (o_ref, l_ref) in enumerate(((o0_ref, l0_ref), (o1_ref, l1_ref), (o2_ref, l2_ref))):
        _interleave(o_ref, obuf.at[g], tmp)
        _interleave(l_ref, lbuf.at[g], tmp)
    l0, l1, l2 = lbuf[0, 0], lbuf[1, 0], lbuf[2, 0]
    m = jnp.maximum(jnp.maximum(l0, l1), l2)
    e0, e1, e2 = jnp.exp(l0 - m), jnp.exp(l1 - m), jnp.exp(l2 - m)
    tot = e0 + e1 + e2
    w0, w1, w2 = e0 / tot, e1 / tot, e2 / tot
    tm = l0.shape[0]
    parts = []
    for h in range(N_HEADS):
        bc = lambda w: jnp.broadcast_to(w[:, h:h + 1], (tm, HEAD_DIM))
        parts.append(bc(w0) * obuf[0, h] + bc(w1) * obuf[1, h] + bc(w2) * obuf[2, h])
    o = jnp.concatenate(parts, axis=1).astype(BF16)
    out_ref[...] = hres_ref[...] + jnp.dot(o, wo_ref[...], preferred_element_type=F32)


def merge_project(outs, lses, h_res, w_o, *, batch):
    t, d = h_res.shape
    seq = t // batch
    hd = N_HEADS * HEAD_DIM
    tm = min(MERGE_TILE, seq)
    per_batch = seq // tm
    row = lambda i: (i, 0)

    def grouped(a):
        dil, width = a.shape[1], a.shape[3]
        return pl.BlockSpec((None, dil, tm // dil, width), lambda i: (i // per_batch, 0, i % per_batch, 0))

    return pl.pallas_call(
        _merge_kernel,
        grid=(t // tm,),
        in_specs=[grouped(a) for a in outs] + [grouped(a) for a in lses] + [
            pl.BlockSpec((tm, d), row),
            pl.BlockSpec((hd, d), lambda i: (0, 0)),
        ],
        out_specs=pl.BlockSpec((tm, d), row),
        out_shape=jax.ShapeDtypeStruct((t, d), F32),
        scratch_shapes=[pltpu.VMEM((len(outs), hd // LANES, tm, LANES), F32),
                        pltpu.VMEM((len(lses), 1, tm, LANES), F32),
                        pltpu.VMEM((tm, LANES), F32)],
        compiler_params=_cparams("parallel"),
        name="merge_project",
    )(*outs, *lses, h_res, w_o)


def _router_kernel(h_ref, g_ref, wr_ref, br_ref, hn_ref, eidx_ref, gate_ref, rank_ref, cnt_ref, carry_ref):
    tm = h_ref.shape[0]
    half = h_ref.shape[1] // 2

    @pl.when(pl.program_id(0) == 0)
    def _():
        carry_ref[...] = jnp.zeros_like(carry_ref)

    hn = _rms(h_ref[...], g_ref[...])
    lo = pltpu.bitcast(hn[:, :half].astype(BF16).astype(F32), jnp.uint32)
    hi = pltpu.bitcast(hn[:, half:].astype(BF16).astype(F32), jnp.uint32)
    hn_ref[...] = hi | (lo >> 16)

    logits = lax.dot_general(wr_ref[...], hn, (((1,), (1,)), ((), ())), precision=lax.Precision.HIGHEST,
                             preferred_element_type=F32) + br_ref[:, 0:1]
    lg = logits[0:N_GROUPS]
    gi = lax.broadcasted_iota(jnp.int32, (N_GROUPS, tm), 0)
    m = jnp.max(lg, axis=0, keepdims=True)
    p_top = 1.0 / jnp.sum(jnp.exp(lg - m), axis=0, keepdims=True)
    g_top = jnp.min(jnp.where(lg == m, gi, N_GROUPS), axis=0, keepdims=True)

    le = jnp.zeros((EXPERTS_PER_GROUP, tm), F32)
    for g in range(N_GROUPS):
        lo_row = N_GROUPS + g * EXPERTS_PER_GROUP
        le = jnp.where(g_top == g, logits[lo_row:lo_row + EXPERTS_PER_GROUP], le)
    ei = lax.broadcasted_iota(jnp.int32, (EXPERTS_PER_GROUP, tm), 0)
    v1 = jnp.max(le, axis=0, keepdims=True)
    i1 = jnp.min(jnp.where(le == v1, ei, EXPERTS_PER_GROUP), axis=0, keepdims=True)
    le2 = jnp.where(ei == i1, -jnp.inf, le)
    v2 = jnp.max(le2, axis=0, keepdims=True)
    i2 = jnp.min(jnp.where(le2 == v2, ei, EXPERTS_PER_GROUP), axis=0, keepdims=True)
    e21 = jnp.exp(v2 - v1)
    gate1 = p_top / (1.0 + e21)
    gate2 = p_top * e21 / (1.0 + e21)
    e1 = g_top * EXPERTS_PER_GROUP + i1
    e2 = g_top * EXPERTS_PER_GROUP + i2
    eidx_ref[...] = jnp.concatenate([e1, e2], axis=0)
    gate_ref[...] = jnp.concatenate([gate1, gate2], axis=0)

    ee = lax.broadcasted_iota(jnp.int32, (N_EXPERTS, tm), 0)
    oh1 = ee == e1
    oh2 = ee == e2
    onehot = jnp.where(oh1 | oh2, 1.0, 0.0)
    tri = jnp.where(lax.broadcasted_iota(jnp.int32, (tm, tm), 0) < lax.broadcasted_iota(jnp.int32, (tm, tm), 1),
                    1.0, 0.0).astype(BF16)
    before = jnp.dot(onehot.astype(BF16), tri, preferred_element_type=F32) + carry_ref[:, 0:1]
    r1 = jnp.sum(jnp.where(oh1, before, 0.0), axis=0, keepdims=True)
    r2 = jnp.sum(jnp.where(oh2, before, 0.0), axis=0, keepdims=True)
    rank_ref[...] = jnp.concatenate([r1, r2], axis=0).astype(jnp.int32)
    carry_ref[...] = carry_ref[...] + jnp.sum(onehot, axis=1, keepdims=True)
    cnt_ref[...] = carry_ref[...].astype(jnp.int32)


def moe_router(h, gain, w_router_t, b_router):
    t, d = h.shape
    tm = min(ROW_TILE, t)
    nlog = w_router_t.shape[0]
    return pl.pallas_call(
        _router_kernel,
        grid=(t // tm,),
        in_specs=[
            pl.BlockSpec((tm, d), lambda i: (i, 0)),
            pl.BlockSpec((1, d), lambda i: (0, 0)),
            pl.BlockSpec((nlog, d), lambda i: (0, 0)),
            pl.BlockSpec((nlog, LANES), lambda i: (0, 0)),
        ],
        out_specs=[
            pl.BlockSpec((tm, d // 2), lambda i: (i, 0)),
            pl.BlockSpec((TOP_K, tm), lambda i: (0, i)),
            pl.BlockSpec((TOP_K, tm), lambda i: (0, i)),
            pl.BlockSpec((TOP_K, tm), lambda i: (0, i)),
            pl.BlockSpec((N_EXPERTS, LANES), lambda i: (0, 0)),
        ],
        out_shape=[
            jax.ShapeDtypeStruct((t, d // 2), jnp.uint32),
            jax.ShapeDtypeStruct((TOP_K, t), jnp.int32),
            jax.ShapeDtypeStruct((TOP_K, t), F32),
            jax.ShapeDtypeStruct((TOP_K, t), jnp.int32),
            jax.ShapeDtypeStruct((N_EXPERTS, LANES), jnp.int32),
        ],
        scratch_shapes=[pltpu.VMEM((N_EXPERTS, LANES), F32)],
        compiler_params=_cparams("arbitrary"),
        name="moe_router",
    )(h, gain.reshape(1, d), w_router_t, b_router)


def _expert_kernel(tok_ref, base_ref, be_ref, first_ref, run_ref, next_ref, used_ref,
                   hn_hbm, wg_hbm, wu_hbm, wd_hbm, o_ref,
                   xbuf, wg_f32, wu_f32, wd_f32, wg_bf, wu_bf, wd_bf, sems, xsems, *, layer):
    i = pl.program_id(0)
    n = pl.num_programs(0)
    rows = xbuf.shape[1]
    n_slot = xbuf.shape[0]
    last_tok = tok_ref.shape[0] - 1
    active = i < used_ref[0]

    def row_copy(blk, slot, r):
        tok = tok_ref[jnp.minimum(base_ref[blk] + r, last_tok)]
        return pltpu.make_async_copy(hn_hbm.at[pl.ds(tok, 1)], xbuf.at[slot, pl.ds(r, 1)], xsems.at[slot])

    def start_rows_loop(blk, slot):
        def body(r, c):
            row_copy(blk, slot, r).start()
            return c
        lax.fori_loop(0, rows, body, 0, unroll=DMA_UNROLL)

    def wait_rows(blk, slot):
        def body(r, c):
            row_copy(blk, slot, r).wait()
            return c
        lax.fori_loop(0, rows, body, 0, unroll=DMA_UNROLL)

    def weight_copies(e, slot):
        return (pltpu.make_async_copy(wg_hbm.at[layer, e], wg_f32.at[slot], sems.at[slot, 0]),
                pltpu.make_async_copy(wu_hbm.at[layer, e], wu_f32.at[slot], sems.at[slot, 1]),
                pltpu.make_async_copy(wd_hbm.at[layer, e], wd_f32.at[slot], sems.at[slot, 2]))

    ahead = jnp.minimum(i + EXPERT_AHEAD, n - 1)
    ahead_slot = (i + EXPERT_AHEAD) % n_slot
    slot_x = i % n_slot

    @pl.when(i == 0)
    def _():
        for c in weight_copies(be_ref[0], 0):
            c.start()
        for s in range(EXPERT_AHEAD):
            start_rows_loop(s, s)

    @pl.when(jnp.logical_and(active, first_ref[i] == 1))
    def _():
        slot = run_ref[i] % 2
        nxt = next_ref[i]

        @pl.when(nxt < N_EXPERTS)
        def _():
            for c in weight_copies(nxt, 1 - slot):
                c.start()

        for c in weight_copies(be_ref[i], slot):
            c.wait()
        wg_bf[...] = wg_f32[slot].astype(BF16)
        wu_bf[...] = wu_f32[slot].astype(BF16)
        wd_bf[...] = wd_f32[slot].astype(BF16)

    wait_rows(i, slot_x)

    @pl.when(active)
    def _():
        xw = xbuf[slot_x]
        lo = pltpu.bitcast(xw << 16, F32).astype(BF16)
        hi = pltpu.bitcast(xw & jnp.uint32(0xFFFF0000), F32).astype(BF16)
        x = jnp.concatenate([lo, hi], axis=1)
        for r in range(rows):
            row_copy(ahead, ahead_slot, r).start()
        g = jnp.dot(x, wg_bf[...], preferred_element_type=F32)
        u = jnp.dot(x, wu_bf[...], preferred_element_type=F32)
        a = (g * _sigmoid(g) * u).astype(BF16)
        o_ref[...] = jnp.dot(a, wd_bf[...], preferred_element_type=F32)

    @pl.when(jnp.logical_not(active))
    def _():
        start_rows_loop(ahead, ahead_slot)
        o_ref[...] = jnp.zeros_like(o_ref)

    @pl.when(i == n - 1)
    def _():
        for s in range(EXPERT_AHEAD):
            wait_rows(n - 1, (n + s) % n_slot)


def moe_experts(order_tok, block_base, block_e, first, run, next_e, n_used, hn_packed, w_gate, w_up, w_down, layer,
                n_blocks):
    _, w = hn_packed.shape
    d = 2 * w
    de = w_gate.shape[3]
    return pl.pallas_call(
        functools.partial(_expert_kernel, layer=layer),
        grid_spec=pltpu.PrefetchScalarGridSpec(
            num_scalar_prefetch=7,
            grid=(n_blocks,),
            in_specs=[pl.BlockSpec(memory_space=pl.ANY)] * 4,
            out_specs=pl.BlockSpec((MOE_BLOCK, d), lambda i, *_: (i, 0)),
            scratch_shapes=[
                pltpu.VMEM((EXPERT_AHEAD + 1, MOE_BLOCK, w), jnp.uint32),
                pltpu.VMEM((2, d, de), F32), pltpu.VMEM((2, d, de), F32), pltpu.VMEM((2, de, d), F32),
                pltpu.VMEM((d, de), BF16), pltpu.VMEM((d, de), BF16), pltpu.VMEM((de, d), BF16),
                pltpu.SemaphoreType.DMA((2, 3)),
                pltpu.SemaphoreType.DMA((EXPERT_AHEAD + 1,)),
            ],
        ),
        out_shape=jax.ShapeDtypeStruct((n_blocks * MOE_BLOCK, d), F32),
        compiler_params=_cparams("arbitrary"),
        name="moe_experts",
    )(order_tok, block_base, block_e, first, run, next_e, n_used, hn_packed, w_gate, w_up, w_down)


def _combine_kernel(dest_ref, h_ref, gates_ref, p_ref, y_hbm, wpg_ref, wpp_ref, pn_ref, o_ref, ybuf, sems):
    st = h_ref.shape[0] // COMBINE_SUBS
    t = dest_ref.shape[0] // TOP_K
    i = pl.program_id(0)
    n_sub = pl.num_programs(0) * COMBINE_SUBS
    n_slot = ybuf.shape[0]

    def row_copy(sub, slot, r, k):
        d = dest_ref[k * t + sub * st + r]
        return pltpu.make_async_copy(y_hbm.at[pl.ds(d, 1)], ybuf.at[slot, k, pl.ds(r, 1)], sems.at[slot])

    def start_sub_loop(sub, slot):
        def body(r, c):
            for k in range(TOP_K):
                row_copy(sub, slot, r, k).start()
            return c
        lax.fori_loop(0, st, body, 0, unroll=DMA_UNROLL)

    def wait_sub(sub, slot):
        def body(r, c):
            for k in range(TOP_K):
                row_copy(sub, slot, r, k).wait()
            return c
        lax.fori_loop(0, st, body, 0, unroll=DMA_UNROLL)

    @pl.when(i == 0)
    def _():
        for s in range(COMBINE_AHEAD):
            start_sub_loop(s, s)

    for j in range(COMBINE_SUBS):
        sub = i * COMBINE_SUBS + j
        slot = sub % n_slot
        wait_sub(sub, slot)
        rows = slice(j * st, (j + 1) * st)
        gates = gates_ref[rows, :]
        h = h_ref[rows, :] + (gates[:, 0:1] * ybuf[slot, 0] + gates[:, 1:2] * ybuf[slot, 1])
        ahead = jnp.minimum(sub + COMBINE_AHEAD, n_sub - 1)
        ahead_slot = (sub + COMBINE_AHEAD) % n_slot
        for r in range(st):
            for k in range(TOP_K):
                row_copy(ahead, ahead_slot, r, k).start()
        gate = _sigmoid(jnp.dot(h.astype(BF16), wpg_ref[...], preferred_element_type=F32))
        e = jnp.dot(p_ref[rows, :].astype(BF16), wpp_ref[...], preferred_element_type=F32)
        o_ref[rows, :] = h + _rms(e * gate, pn_ref[...])

    @pl.when(i == pl.num_programs(0) - 1)
    def _():
        for s in range(COMBINE_AHEAD):
            wait_sub(n_sub - 1, (n_sub + s) % n_slot)


def moe_combine_ple(dest_flat, h, gates_tk, p, yb, w_ple_gate, w_ple_proj, ple_gain):
    t, d = h.shape
    tm = min(COMBINE_SUB * COMBINE_SUBS, t)
    sub = tm // COMBINE_SUBS
    pd = p.shape[1]
    row = lambda i, dest: (i, 0)
    const = lambda i, dest: (0, 0)
    resident = pl.Buffered(1)
    return pl.pallas_call(
        _combine_kernel,
        grid_spec=pltpu.PrefetchScalarGridSpec(
            num_scalar_prefetch=1,
            grid=(t // tm,),
            in_specs=[
                pl.BlockSpec((tm, d), row),
                pl.BlockSpec((tm, TOP_K), row),
                pl.BlockSpec((tm, pd), row),
                pl.BlockSpec(memory_space=pl.ANY),
                pl.BlockSpec((d, d), const, pipeline_mode=resident),
                pl.BlockSpec((pd, d), const, pipeline_mode=resident),
                pl.BlockSpec((1, d), const),
            ],
            out_specs=pl.BlockSpec((tm, d), row),
            scratch_shapes=[pltpu.VMEM((COMBINE_SUBS + COMBINE_AHEAD, TOP_K, sub, d), F32),
                            pltpu.SemaphoreType.DMA((COMBINE_SUBS + COMBINE_AHEAD,))],
        ),
        out_shape=jax.ShapeDtypeStruct((t, d), F32),
        compiler_params=_cparams("arbitrary"),
        name="moe_combine_ple",
    )(dest_flat, h, gates_tk, p, yb, w_ple_gate, w_ple_proj, ple_gain.reshape(1, d))


def _moe_ple_layer(h, p_i, norm_ffn, w_group, b_group, w_expert, b_expert, w_gate, w_up, w_down, layer,
                   ple_w_proj, ple_w_gate, ple_norm):
    t, d = h.shape
    ng, _, epg = w_expert.shape
    w_rt = jnp.concatenate([w_group.T, jnp.transpose(w_expert, (0, 2, 1)).reshape(ng * epg, d)], axis=0)
    b_r = jnp.concatenate([b_group, b_expert.reshape(-1)])
    pad = LANES - w_rt.shape[0]
    w_rt = jnp.pad(w_rt, ((0, pad), (0, 0)))
    b_r = jnp.broadcast_to(jnp.pad(b_r, (0, pad))[:, None], (LANES, LANES))

    hn_packed, eidx, gates, rank, counts = moe_router(h, norm_ffn, w_rt, b_r)

    counts = counts[:, 0]
    padded = (counts + MOE_BLOCK - 1) // MOE_BLOCK * MOE_BLOCK
    pad_end = jnp.cumsum(padded)
    pad_start = pad_end - padded
    n_blocks = (t * TOP_K + N_EXPERTS * (MOE_BLOCK - 1) + MOE_BLOCK - 1) // MOE_BLOCK
    expert_ids = jnp.arange(N_EXPERTS, dtype=jnp.int32)
    run_start = jnp.sum(jnp.where(eidx[..., None] == expert_ids, pad_start, 0), axis=-1)
    dest = (run_start + rank).reshape(-1).astype(jnp.int32)
    block_start = jnp.arange(n_blocks, dtype=jnp.int32) * MOE_BLOCK
    block_e = jnp.minimum(jnp.sum((pad_end[None, :] <= block_start[:, None]).astype(jnp.int32), axis=1),
                          N_EXPERTS - 1)
    first = jnp.concatenate([jnp.ones((1,), jnp.int32), (block_e[1:] != block_e[:-1]).astype(jnp.int32)])
    run = jnp.cumsum(first) - 1
    later_nonempty = (expert_ids[None, :] > expert_ids[:, None]) & (padded[None, :] > 0)
    next_of = jnp.min(jnp.where(later_nonempty, expert_ids[None, :], N_EXPERTS), axis=1)
    next_e = jnp.sum(jnp.where(block_e[:, None] == expert_ids, next_of, 0), axis=-1)
    n_used = (pad_end[-1:] // MOE_BLOCK).astype(jnp.int32)
    dense_start = jnp.cumsum(counts) - counts
    dense_pos = jnp.sum(jnp.where(eidx[..., None] == expert_ids, dense_start, 0), axis=-1) + rank
    order_tok = (jnp.argsort(dense_pos.reshape(-1)) % t).astype(jnp.int32)
    gap = pad_start - dense_start
    block_base = block_start - jnp.sum(jnp.where(block_e[:, None] == expert_ids, gap, 0), axis=-1)

    yb = moe_experts(order_tok, block_base.astype(jnp.int32), block_e, first, run.astype(jnp.int32),
                     next_e.astype(jnp.int32), n_used, hn_packed, w_gate, w_up, w_down, layer, n_blocks)
    return moe_combine_ple(dest, h, gates.T, p_i, yb, ple_w_gate.astype(BF16), ple_w_proj.astype(BF16), ple_norm)


def _lru_layer(h, batch, norm, w_in, conv_w, conv_b, w_rg, b_rg, w_ig, b_ig, lam, w_out):
    w_gates = jnp.concatenate([w_rg, w_ig], axis=-1).astype(BF16)
    return lru_core(h, norm, w_in.astype(BF16), conv_w, conv_b, w_gates, b_rg.reshape(-1), b_ig.reshape(-1), lam,
                    w_out.astype(BF16), batch=batch)


def _attn_layer(h, batch, cos, sin, norm, w_qkv, q_gain, k_gain, w_o):
    w_qkv = w_qkv.astype(BF16)
    outs, lses = [], []
    for g, (_, dilation) in enumerate(DILATION_PAIRS):
        qkv_g = qkv_project(h, norm, w_qkv, cos, sin, q_gain[g], k_gain[g], g, dilation, batch=batch)
        o, lse = dilated_attention(qkv_g)
        outs.append(o)
        lses.append(lse)
    return merge_project(outs, lses, h, w_o.astype(BF16), batch=batch)


def kernel(x, p, positions, norm_mix, norm_ffn, lru_w_in, lru_conv_w, lru_conv_b, lru_w_rg, lru_b_rg, lru_w_ig, lru_b_ig, lru_lambda, lru_w_out, attn_w_qkv, attn_q_gain, attn_k_gain, attn_w_o, moe_w_group, moe_b_group, moe_w_expert, moe_b_expert, moe_w_gate, moe_w_up, moe_w_down, ple_w_proj, ple_w_gate, ple_norm):
    batch, seq, d = x.shape
    depth = p.shape[0]
    t = batch * seq
    h = x.reshape(t, d)
    cos, sin = rope_tables(positions.reshape(t))
    for i in range(depth):
        j = i // 2
        if i % 2 == 0:
            h = _lru_layer(h, batch, norm_mix[i], lru_w_in[j], lru_conv_w[j], lru_conv_b[j], lru_w_rg[j],
                           lru_b_rg[j], lru_w_ig[j], lru_b_ig[j], lru_lambda[j], lru_w_out[j])
        else:
            h = _attn_layer(h, batch, cos, sin, norm_mix[i], attn_w_qkv[j], attn_q_gain[j], attn_k_gain[j],
                            attn_w_o[j])
        h = _moe_ple_layer(h, p[i].reshape(t, -1), norm_ffn[i], moe_w_group[i], moe_b_group[i],
                           moe_w_expert[i], moe_b_expert[i], moe_w_gate, moe_w_up, moe_w_down, i,
                           ple_w_proj[i], ple_w_gate[i], ple_norm[i])
    return h.reshape(batch, seq, d)
```

```python
import functools

import jax
import jax.numpy as jnp
from jax import lax
from jax.experimental import pallas as pl
from jax.experimental.pallas import tpu as pltpu

NORM_EPS = 1e-6
N_LRU_BLOCKS = 8
CONV_WIDTH = 4
LRU_C = 8.0
DILATION_PAIRS = ((128, 1), (512, 4), (2048, 16))
N_HEADS = 8
HEAD_DIM = 128
ATTN_BLOCK = 128
ROPE_THETA = 10000.0
NEG_INF = -1e30
N_GROUPS = 8
EXPERTS_PER_GROUP = 8
N_EXPERTS = N_GROUPS * EXPERTS_PER_GROUP
TOP_K = 2

LANES = 128
SUBLANES = 8
VMEM_LIMIT_BYTES = 56 * 1024 * 1024

ROW_TILE = 512
LRU_CHUNK = 256
ATTN_TILE = 512
MOE_BLOCK = 256
EXPERT_AHEAD = 3
WEIGHT_DMA_PRIORITY = 1
QKV_TILE = 256
QKV_CHUNK = 256
MERGE_TILE = 256
DMA_UNROLL = 8
COMBINE_SUB = 256
COMBINE_SUBS = 2
COMBINE_AHEAD = 2

F32 = jnp.float32
BF16 = jnp.bfloat16


def _cparams(*sem):
    return pltpu.CompilerParams(dimension_semantics=sem, vmem_limit_bytes=VMEM_LIMIT_BYTES)


def _rms(x, gain):
    ms = jnp.mean(x * x, axis=-1, keepdims=True)
    return x * lax.rsqrt(ms + NORM_EPS) * gain


def _sigmoid(x):
    return 0.5 * jnp.tanh(0.5 * x) + 0.5


def _shift_rows_in_vreg(x, s, fill):
    g = x.shape[0] // SUBLANES
    x3 = x.reshape(g, SUBLANES, x.shape[1])
    rolled = pltpu.roll(x3, s, axis=1)
    row = lax.broadcasted_iota(jnp.int32, x3.shape, 1)
    return jnp.where(row >= s, rolled, fill).reshape(x.shape)


def _shift_rows(x, s, fill):
    row = lax.broadcasted_iota(jnp.int32, x.shape, 0)
    return jnp.where(row >= s, pltpu.roll(x, s, axis=0), fill)


def _one_minus_exp(x, ex):
    near = jnp.where(ex == 1.0, -x, (1.0 - ex) * x / jnp.log(ex))
    return jnp.where(x > -0.5, near, 1.0 - ex)


def _lru_kernel(h_ref, gmix_ref, win_ref, cw_ref, cb_ref, wg_ref, brg_ref, big_ref, lam_ref, wo_ref,
                o_ref, xn_ref, xprev_ref, hcarry_ref, xc_ref, a_ref, u_ref, hin_ref, tot_ref):
    tc, d = h_ref.shape
    blk = d // N_LRU_BLOCKS

    @pl.when(pl.program_id(1) == 0)
    def _():
        xprev_ref[...] = jnp.zeros_like(xprev_ref)
        hcarry_ref[...] = jnp.zeros_like(hcarry_ref)

    xn_ref[...] = _rms(h_ref[...], gmix_ref[...]).astype(BF16)
    x = jnp.dot(xn_ref[...], win_ref[:, d:], preferred_element_type=F32)

    prev = xprev_ref[...]
    cw = cw_ref[...]
    acc = cb_ref[...] + cw[CONV_WIDTH - 1:CONV_WIDTH] * x
    head = cb_ref[...] + cw[CONV_WIDTH - 1:CONV_WIDTH] * x[0:SUBLANES]
    row8 = lax.broadcasted_iota(jnp.int32, (SUBLANES, d), 0)
    for s in range(1, CONV_WIDTH):
        wk = cw[CONV_WIDTH - 1 - s:CONV_WIDTH - s]
        acc = acc + wk * pltpu.roll(x, s, axis=0)
        first = jnp.where(row8 >= s, pltpu.roll(x[0:SUBLANES], s, axis=0), pltpu.roll(prev, s, axis=0))
        head = head + wk * first
    xc_ref[...] = acc
    xc_ref[0:SUBLANES, :] = head
    xprev_ref[...] = x[tc - SUBLANES:tc]

    lam = lam_ref[...]
    log_sig = jnp.minimum(lam, 0.0) - jnp.log1p(jnp.exp(-jnp.abs(lam)))

    for n in range(N_LRU_BLOCKS):
        sl = slice(n * blk, (n + 1) * blk)
        xc = xc_ref[:, sl]
        z = jnp.dot(xc.astype(BF16), wg_ref[n], preferred_element_type=F32)
        r = _sigmoid(z[:, :blk] + brg_ref[:, sl])
        i = _sigmoid(z[:, blk:] + big_ref[:, sl])
        log_a = LRU_C * r * log_sig[:, sl]
        a = jnp.exp(log_a)
        a_ref[:, sl] = a
        u_ref[:, sl] = jnp.sqrt(_one_minus_exp(2.0 * log_a, a * a)) * (i * xc)

    a = a_ref[...]
    u = u_ref[...]
    for s in (1, 2, 4):
        a_sh = _shift_rows_in_vreg(a, s, 1.0)
        u_sh = _shift_rows_in_vreg(u, s, 0.0)
        u = a * u_sh + u
        a = a * a_sh
    a_ref[...] = a
    u_ref[...] = u
    ng = tc // SUBLANES
    for j in range(ng):
        last = slice(j * SUBLANES + SUBLANES - 1, (j + 1) * SUBLANES)
        hin_ref[j:j + 1, :] = a_ref[last, :]
        tot_ref[j:j + 1, :] = u_ref[last, :]
    pa = hin_ref[...]
    su = tot_ref[...]
    s = 1
    while s < ng:
        su = pa * _shift_rows(su, s, 0.0) + su
        pa = pa * _shift_rows(pa, s, 1.0)
        s *= 2
    h0 = hcarry_ref[...]
    after = pa * h0 + su
    hcarry_ref[...] = after[ng - 1:ng]
    row = lax.broadcasted_iota(jnp.int32, after.shape, 0)
    hin_ref[...] = jnp.where(row == 0, h0, pltpu.roll(after, 1, axis=0))
    for j in range(ng):
        rows = slice(j * SUBLANES, (j + 1) * SUBLANES)
        h_in = jnp.broadcast_to(hin_ref[j:j + 1, :], (SUBLANES, d))
        u_ref[rows, :] = a_ref[rows, :] * h_in + u_ref[rows, :]

    gate = jnp.dot(xn_ref[...], win_ref[:, :d], preferred_element_type=F32)
    y = jax.nn.gelu(gate, approximate=True) * u_ref[...]
    o_ref[...] = h_ref[...] + jnp.dot(y.astype(BF16), wo_ref[...], preferred_element_type=F32)


def lru_core(h, gain, w_in, conv_w, conv_b, w_gates, b_rg, b_ig, lam, w_out, *, batch):
    t, d = h.shape
    seq = t // batch
    tc = min(LRU_CHUNK, seq)
    nc = seq // tc
    blk = d // N_LRU_BLOCKS
    row = lambda b, c: (b * nc + c, 0)
    const2 = lambda b, c: (0, 0)
    resident = pl.Buffered(1)
    return pl.pallas_call(
        _lru_kernel,
        grid=(batch, nc),
        in_specs=[
            pl.BlockSpec((tc, d), row),
            pl.BlockSpec((1, d), const2),
            pl.BlockSpec((d, 2 * d), const2, pipeline_mode=resident),
            pl.BlockSpec((CONV_WIDTH, d), const2),
            pl.BlockSpec((1, d), const2),
            pl.BlockSpec((N_LRU_BLOCKS, blk, 2 * blk), lambda b, c: (0, 0, 0), pipeline_mode=resident),
            pl.BlockSpec((1, d), const2),
            pl.BlockSpec((1, d), const2),
            pl.BlockSpec((1, d), const2),
            pl.BlockSpec((d, d), const2, pipeline_mode=resident),
        ],
        out_specs=pl.BlockSpec((tc, d), row),
        out_shape=jax.ShapeDtypeStruct((t, d), F32),
        scratch_shapes=[
            pltpu.VMEM((tc, d), BF16),
            pltpu.VMEM((SUBLANES, d), F32),
            pltpu.VMEM((1, d), F32),
            pltpu.VMEM((tc, d), F32),
            pltpu.VMEM((tc, d), F32),
            pltpu.VMEM((tc, d), F32),
            pltpu.VMEM((tc // SUBLANES, d), F32),
            pltpu.VMEM((tc // SUBLANES, d), F32),
        ],
        compiler_params=_cparams("parallel", "arbitrary"),
        name="lru_core",
    )(h, gain.reshape(1, d), w_in, conv_w, conv_b.reshape(1, d), w_gates, b_rg.reshape(1, d), b_ig.reshape(1, d),
      lam.reshape(1, d), w_out)


def _rope_kernel(pos_ref, inv_ref, cos_ref, sin_ref):
    ang = pos_ref[...].astype(F32) * inv_ref[...]
    lane = lax.broadcasted_iota(jnp.int32, ang.shape, 1)
    cos_ref[...] = jnp.cos(ang)
    sin_ref[...] = jnp.where(lane < HEAD_DIM // 2, -jnp.sin(ang), jnp.sin(ang))


def rope_tables(positions_flat):
    t = positions_flat.shape[0]
    tm = min(ROW_TILE, t)
    inv = ROPE_THETA ** (-jnp.arange(0, HEAD_DIM, 2, dtype=F32) / HEAD_DIM)
    inv2 = jnp.concatenate([inv, inv]).reshape(1, HEAD_DIM)
    return pl.pallas_call(
        _rope_kernel,
        grid=(t // tm,),
        in_specs=[pl.BlockSpec((tm, 1), lambda i: (i, 0)), pl.BlockSpec((1, HEAD_DIM), lambda i: (0, 0))],
        out_specs=[pl.BlockSpec((tm, HEAD_DIM), lambda i: (i, 0))] * 2,
        out_shape=[jax.ShapeDtypeStruct((t, HEAD_DIM), F32)] * 2,
        compiler_params=_cparams("parallel"),
        name="rope_tables",
    )(positions_flat.reshape(t, 1), inv2)


def _qkv_kernel(x_ref, g_ref, w_ref, cos_ref, sin_ref, qg_ref, kg_ref, o_ref, stage_ref, stage2_ref, *, dilation):
    assert dilation in (1, 4, 16)
    tm = x_ref.shape[0]
    hd = N_HEADS * HEAD_DIM
    xn = _rms(x_ref[...], g_ref[...]).astype(BF16)
    cos = cos_ref[...]
    sin = sin_ref[...]
    for kind in range(3):
        for c in range(hd // QKV_CHUNK):
            cols = slice(kind * hd + c * QKV_CHUNK, kind * hd + (c + 1) * QKV_CHUNK)
            z = jnp.dot(xn, w_ref[:, cols], preferred_element_type=F32)
            if kind < 2:
                gain = (qg_ref if kind == 0 else kg_ref)[...]
                heads = []
                for hh in range(QKV_CHUNK // HEAD_DIM):
                    y = _rms(z[:, hh * HEAD_DIM:(hh + 1) * HEAD_DIM], gain)
                    y = y * cos + pltpu.roll(y, HEAD_DIM // 2, axis=1) * sin
                    heads.append(y * (HEAD_DIM ** -0.5) if kind == 0 else y)
                z = jnp.concatenate(heads, axis=1)
            if dilation == 1:
                o_ref[0, :, cols] = z.astype(BF16)
            else:
                for hh in range(QKV_CHUNK // LANES):
                    stage_ref[cols.start // LANES + hh] = z[:, hh * LANES:(hh + 1) * LANES]
    if dilation == 4:
        for r in range(dilation):
            for cc in range(3 * hd // LANES):
                o_ref[r, :, cc * LANES:(cc + 1) * LANES] = (
                    stage_ref[cc, pl.ds(r, tm // dilation, stride=dilation), :].astype(BF16))
    if dilation == 16:
        quarter = tm // 4
        for cc in range(3 * hd // LANES):
            for r1 in range(4):
                stage2_ref[cc, r1 * quarter:(r1 + 1) * quarter, :] = stage_ref[cc, pl.ds(r1, quarter, stride=4), :]
            for r1 in range(4):
                for r2 in range(4):
                    o_ref[r1 + 4 * r2, :, cc * LANES:(cc + 1) * LANES] = (
                        stage2_ref[cc, pl.ds(r1 * quarter + r2, tm // dilation, stride=4), :].astype(BF16))


def qkv_project(x, gain, w_qkv, cos, sin, q_gain, k_gain, group, dilation, *, batch):
    t, k = x.shape
    seq = t // batch
    sub = seq // dilation
    hd3 = 3 * N_HEADS * HEAD_DIM
    tm = min(QKV_TILE, seq)
    per_batch = seq // tm
    row = lambda i: (i, 0)
    const = lambda i: (0, 0)
    return pl.pallas_call(
        functools.partial(_qkv_kernel, dilation=dilation),
        grid=(t // tm,),
        in_specs=[
            pl.BlockSpec((tm, k), row),
            pl.BlockSpec((1, k), const),
            pl.BlockSpec((k, hd3), lambda i: (0, group)),
            pl.BlockSpec((tm, HEAD_DIM), row),
            pl.BlockSpec((tm, HEAD_DIM), row),
            pl.BlockSpec((1, HEAD_DIM), const),
            pl.BlockSpec((1, HEAD_DIM), const),
        ],
        out_specs=pl.BlockSpec((None, dilation, tm // dilation, hd3),
                               lambda i: (i // per_batch, 0, i % per_batch, 0)),
        out_shape=jax.ShapeDtypeStruct((batch, dilation, sub, hd3), BF16),
        scratch_shapes=[pltpu.VMEM((hd3 // LANES, tm if dilation > 1 else SUBLANES, LANES), F32),
                        pltpu.VMEM((hd3 // LANES, tm if dilation == 16 else SUBLANES, LANES), F32)],
        compiler_params=_cparams("parallel"),
        name=f"qkv_project_d{dilation}",
    )(x, gain.reshape(1, k), w_qkv, cos, sin, q_gain.reshape(1, HEAD_DIM), k_gain.reshape(1, HEAD_DIM))


def _attn_kernel(q_ref, kc_ref, kp_ref, vc_ref, vp_ref, o_ref, lse_ref, kbuf_ref, vbuf_ref):
    tl = q_ref.shape[0]
    blk = ATTN_BLOCK
    first_tile = pl.program_id(2) == 0

    kbuf_ref[0:blk, :] = kp_ref[...]
    kbuf_ref[blk:, :] = kc_ref[...]
    vbuf_ref[0:blk, :] = vp_ref[...]
    vbuf_ref[blk:, :] = vc_ref[...]

    qi = lax.broadcasted_iota(jnp.int32, (blk, 2 * blk), 0)
    kj = lax.broadcasted_iota(jnp.int32, (blk, 2 * blk), 1)
    upper = kj <= qi + blk
    bias_gen = jnp.where(upper & (kj >= qi), 0.0, NEG_INF).astype(F32)
    bias_first = jnp.where(upper & (kj >= blk), 0.0, NEG_INF).astype(F32)
    bias0 = jnp.where(first_tile, bias_first, bias_gen)
    lane = lax.broadcasted_iota(jnp.int32, (blk, LANES), 1)

    for qb in range(tl // blk):
        bias = bias0 if qb == 0 else bias_gen
        rows = slice(qb * blk, (qb + 1) * blk)
        win = slice(qb * blk, (qb + 2) * blk)
        lse_tile = jnp.zeros((blk, LANES), F32)
        for h in range(N_HEADS):
            cols = slice(h * HEAD_DIM, (h + 1) * HEAD_DIM)
            s = lax.dot_general(q_ref[rows, cols], kbuf_ref[win, cols], (((1,), (1,)), ((), ())),
                                preferred_element_type=F32) + bias
            m = jnp.max(s, axis=-1, keepdims=True)
            p = jnp.exp(s - m)
            den = jnp.sum(p, axis=-1, keepdims=True)
            o = jnp.dot(p.astype(BF16), vbuf_ref[win, cols], preferred_element_type=F32)
            o_ref[rows, cols] = o / den
            lse_tile = jnp.where(lane == h, m + jnp.log(den), lse_tile)
        lse_ref[rows, :] = lse_tile


def dilated_attention(qkv_g):
    batch, dilation, sub, hd3 = qkv_g.shape
    hd = hd3 // 3
    tl = min(ATTN_TILE, sub)
    per = tl // ATTN_BLOCK
    cur = lambda kind: (lambda b, r, i: (b, r, i, kind))
    prev = lambda kind: (lambda b, r, i: (b, r, jnp.maximum(i * per - 1, 0), kind))
    return pl.pallas_call(
        _attn_kernel,
        grid=(batch, dilation, sub // tl),
        in_specs=[
            pl.BlockSpec((None, None, tl, hd), cur(0)),
            pl.BlockSpec((None, None, tl, hd), cur(1)),
            pl.BlockSpec((None, None, ATTN_BLOCK, hd), prev(1)),
            pl.BlockSpec((None, None, tl, hd), cur(2)),
            pl.BlockSpec((None, None, ATTN_BLOCK, hd), prev(2)),
        ],
        out_specs=[
            pl.BlockSpec((None, None, tl, hd), lambda b, r, i: (b, r, i, 0)),
            pl.BlockSpec((None, None, tl, LANES), lambda b, r, i: (b, r, i, 0)),
        ],
        out_shape=[
            jax.ShapeDtypeStruct((batch, dilation, sub, hd), F32),
            jax.ShapeDtypeStruct((batch, dilation, sub, LANES), F32),
        ],
        scratch_shapes=[pltpu.VMEM((tl + ATTN_BLOCK, hd), BF16), pltpu.VMEM((tl + ATTN_BLOCK, hd), BF16)],
        compiler_params=_cparams("parallel", "parallel", "arbitrary"),
        name=f"dilated_attention_d{dilation}",
    )(qkv_g, qkv_g, qkv_g, qkv_g, qkv_g)


def _interleave(src_ref, dst_ref, tmp_ref):
    dilation, rows, width = src_ref.shape
    assert dilation in (1, 4, 16)
    for c in range(width // LANES):
        cols = slice(c * LANES, (c + 1) * LANES)
        if dilation == 1:
            dst_ref[c] = src_ref[0, :, cols]
        elif dilation == 4:
            for r in range(dilation):
                dst_ref[c, pl.ds(r, rows, stride=dilation), :] = src_ref[r, :, cols]
        else:
            quarter = 4 * rows
            for r1 in range(4):
                for r2 in range(4):
                    tmp_ref[pl.ds(r1 * quarter + r2, rows, stride=4), :] = src_ref[r1 + 4 * r2, :, cols]
            for r1 in range(4):
                dst_ref[c, pl.ds(r1, quarter, stride=4), :] = tmp_ref[r1 * quarter:(r1 + 1) * quarter, :]


def _merge_kernel(o0_ref, o1_ref, o2_ref, l0_ref, l1_ref, l2_ref, hres_ref, wo_ref, out_ref, obuf, lbuf, tmp):
    for g, (o_ref, l_ref) in enumerate(((o0_ref, l0_ref), (o1_ref, l1_ref), (o2_ref, l2_ref))):
        _interleave(o_ref, obuf.at[g], tmp)
        _interleave(l_ref, lbuf.at[g], tmp)
    l0, l1, l2 = lbuf[0, 0], lbuf[1, 0], lbuf[2, 0]
    m = jnp.maximum(jnp.maximum(l0, l1), l2)
    e0, e1, e2 = jnp.exp(l0 - m), jnp.exp(l1 - m), jnp.exp(l2 - m)
    tot = e0 + e1 + e2
    w0, w1, w2 = e0 / tot, e1 / tot, e2 / tot
    tm = l0.shape[0]
    parts = []
    for h in range(N_HEADS):
        bc = lambda w: jnp.broadcast_to(w[:, h:h + 1], (tm, HEAD_DIM))
        parts.append(bc(w0) * obuf[0, h] + bc(w1) * obuf[1, h] + bc(w2) * obuf[2, h])
    o = jnp.concatenate(parts, axis=1).astype(BF16)
    out_ref[...] = hres_ref[...] + jnp.dot(o, wo_ref[...], preferred_element_type=F32)


def merge_project(outs, lses, h_res, w_o, *, batch):
    t, d = h_res.shape
    seq = t // batch
    hd = N_HEADS * HEAD_DIM
    tm = min(MERGE_TILE, seq)
    per_batch = seq // tm
    row = lambda i: (i, 0)

    def grouped(a):
        dil, width = a.shape[1], a.shape[3]
        return pl.BlockSpec((None, dil, tm // dil, width), lambda i: (i // per_batch, 0, i % per_batch, 0))

    return pl.pallas_call(
        _merge_kernel,
        grid=(t // tm,),
        in_specs=[grouped(a) for a in outs] + [grouped(a) for a in lses] + [
            pl.BlockSpec((tm, d), row),
            pl.BlockSpec((hd, d), lambda i: (0, 0)),
        ],
        out_specs=pl.BlockSpec((tm, d), row),
        out_shape=jax.ShapeDtypeStruct((t, d), F32),
        scratch_shapes=[pltpu.VMEM((len(outs), hd // LANES, tm, LANES), F32),
                        pltpu.VMEM((len(lses), 1, tm, LANES), F32),
                        pltpu.VMEM((tm, LANES), F32)],
        compiler_params=_cparams("parallel"),
        name="merge_project",
    )(*outs, *lses, h_res, w_o)


def _router_kernel(h_ref, g_ref, wr_ref, br_ref, hn_ref, eidx_ref, gate_ref, rank_ref, cnt_ref, carry_ref):
    tm = h_ref.shape[0]
    half = h_ref.shape[1] // 2

    @pl.when(pl.program_id(0) == 0)
    def _():
        carry_ref[...] = jnp.zeros_like(carry_ref)

    hn = _rms(h_ref[...], g_ref[...])
    lo = pltpu.bitcast(hn[:, :half].astype(BF16).astype(F32), jnp.uint32)
    hi = pltpu.bitcast(hn[:, half:].astype(BF16).astype(F32), jnp.uint32)
    hn_ref[...] = hi | (lo >> 16)

    logits = lax.dot_general(wr_ref[...], hn, (((1,), (1,)), ((), ())), precision=lax.Precision.HIGHEST,
                             preferred_element_type=F32) + br_ref[:, 0:1]
    lg = logits[0:N_GROUPS]
    gi = lax.broadcasted_iota(jnp.int32, (N_GROUPS, tm), 0)
    m = jnp.max(lg, axis=0, keepdims=True)
    p_top = 1.0 / jnp.sum(jnp.exp(lg - m), axis=0, keepdims=True)
    g_top = jnp.min(jnp.where(lg == m, gi, N_GROUPS), axis=0, keepdims=True)

    le = jnp.zeros((EXPERTS_PER_GROUP, tm), F32)
    for g in range(N_GROUPS):
        lo_row = N_GROUPS + g * EXPERTS_PER_GROUP
        le = jnp.where(g_top == g, logits[lo_row:lo_row + EXPERTS_PER_GROUP], le)
    ei = lax.broadcasted_iota(jnp.int32, (EXPERTS_PER_GROUP, tm), 0)
    v1 = jnp.max(le, axis=0, keepdims=True)
    i1 = jnp.min(jnp.where(le == v1, ei, EXPERTS_PER_GROUP), axis=0, keepdims=True)
    le2 = jnp.where(ei == i1, -jnp.inf, le)
    v2 = jnp.max(le2, axis=0, keepdims=True)
    i2 = jnp.min(jnp.where(le2 == v2, ei, EXPERTS_PER_GROUP), axis=0, keepdims=True)
    e21 = jnp.exp(v2 - v1)
    gate1 = p_top / (1.0 + e21)
    gate2 = p_top * e21 / (1.0 + e21)
    e1 = g_top * EXPERTS_PER_GROUP + i1
    e2 = g_top * EXPERTS_PER_GROUP + i2
    eidx_ref[...] = jnp.concatenate([e1, e2], axis=0)
    gate_ref[...] = jnp.concatenate([gate1, gate2], axis=0)

    ee = lax.broadcasted_iota(jnp.int32, (N_EXPERTS, tm), 0)
    oh1 = ee == e1
    oh2 = ee == e2
    onehot = jnp.where(oh1 | oh2, 1.0, 0.0)
    tri = jnp.where(lax.broadcasted_iota(jnp.int32, (tm, tm), 0) < lax.broadcasted_iota(jnp.int32, (tm, tm), 1),
                    1.0, 0.0).astype(BF16)
    before = jnp.dot(onehot.astype(BF16), tri, preferred_element_type=F32) + carry_ref[:, 0:1]
    r1 = jnp.sum(jnp.where(oh1, before, 0.0), axis=0, keepdims=True)
    r2 = jnp.sum(jnp.where(oh2, before, 0.0), axis=0, keepdims=True)
    rank_ref[...] = jnp.concatenate([r1, r2], axis=0).astype(jnp.int32)
    carry_ref[...] = carry_ref[...] + jnp.sum(onehot, axis=1, keepdims=True)
    cnt_ref[...] = carry_ref[...].astype(jnp.int32)


def moe_router(h, gain, w_router_t, b_router):
    t, d = h.shape
    tm = min(ROW_TILE, t)
    nlog = w_router_t.shape[0]
    return pl.pallas_call(
        _router_kernel,
        grid=(t // tm,),
        in_specs=[
            pl.BlockSpec((tm, d), lambda i: (i, 0)),
            pl.BlockSpec((1, d), lambda i: (0, 0)),
            pl.BlockSpec((nlog, d), lambda i: (0, 0)),
            pl.BlockSpec((nlog, LANES), lambda i: (0, 0)),
        ],
        out_specs=[
            pl.BlockSpec((tm, d // 2), lambda i: (i, 0)),
            pl.BlockSpec((TOP_K, tm), lambda i: (0, i)),
            pl.BlockSpec((TOP_K, tm), lambda i: (0, i)),
            pl.BlockSpec((TOP_K, tm), lambda i: (0, i)),
            pl.BlockSpec((N_EXPERTS, LANES), lambda i: (0, 0)),
        ],
        out_shape=[
            jax.ShapeDtypeStruct((t, d // 2), jnp.uint32),
            jax.ShapeDtypeStruct((TOP_K, t), jnp.int32),
            jax.ShapeDtypeStruct((TOP_K, t), F32),
            jax.ShapeDtypeStruct((TOP_K, t), jnp.int32),
            jax.ShapeDtypeStruct((N_EXPERTS, LANES), jnp.int32),
        ],
        scratch_shapes=[pltpu.VMEM((N_EXPERTS, LANES), F32)],
        compiler_params=_cparams("arbitrary"),
        name="moe_router",
    )(h, gain.reshape(1, d), w_router_t, b_router)


def _expert_kernel(tok_ref, base_ref, be_ref, first_ref, run_ref, next_ref, used_ref,
                   hn_hbm, wg_hbm, wu_hbm, wd_hbm, o_ref,
                   xbuf, wg_f32, wu_f32, wd_f32, wg_bf, wu_bf, wd_bf, sems, xsems, *, layer):
    i = pl.program_id(0)
    n = pl.num_programs(0)
    rows = xbuf.shape[1]
    n_slot = xbuf.shape[0]
    last_tok = tok_ref.shape[0] - 1
    active = i < used_ref[0]

    def row_copy(blk, slot, r):
        tok = tok_ref[jnp.minimum(base_ref[blk] + r, last_tok)]
        return pltpu.make_async_copy(hn_hbm.at[pl.ds(tok, 1)], xbuf.at[slot, pl.ds(r, 1)], xsems.at[slot])

    def start_rows_loop(blk, slot):
        def body(r, c):
            row_copy(blk, slot, r).start()
            return c
        lax.fori_loop(0, rows, body, 0, unroll=DMA_UNROLL)

    def wait_rows(blk, slot):
        def body(r, c):
            row_copy(blk, slot, r).wait()
            return c
        lax.fori_loop(0, rows, body, 0, unroll=DMA_UNROLL)

    def weight_copies(e, slot):
        return (pltpu.make_async_copy(wg_hbm.at[layer, e], wg_f32.at[slot], sems.at[slot, 0]),
                pltpu.make_async_copy(wu_hbm.at[layer, e], wu_f32.at[slot], sems.at[slot, 1]),
                pltpu.make_async_copy(wd_hbm.at[layer, e], wd_f32.at[slot], sems.at[slot, 2]))

    ahead = jnp.minimum(i + EXPERT_AHEAD, n - 1)
    ahead_slot = (i + EXPERT_AHEAD) % n_slot
    slot_x = i % n_slot

    @pl.when(i == 0)
    def _():
        for c in weight_copies(be_ref[0], 0):
            c.start(priority=WEIGHT_DMA_PRIORITY)
        for s in range(EXPERT_AHEAD):
            start_rows_loop(s, s)

    @pl.when(jnp.logical_and(active, first_ref[i] == 1))
    def _():
        slot = run_ref[i] % 2
        nxt = next_ref[i]

        @pl.when(nxt < N_EXPERTS)
        def _():
            for c in weight_copies(nxt, 1 - slot):
                c.start(priority=WEIGHT_DMA_PRIORITY)

        for c in weight_copies(be_ref[i], slot):
            c.wait()
        wg_bf[...] = wg_f32[slot].astype(BF16)
        wu_bf[...] = wu_f32[slot].astype(BF16)
        wd_bf[...] = wd_f32[slot].astype(BF16)

    wait_rows(i, slot_x)

    @pl.when(active)
    def _():
        xw = xbuf[slot_x]
        lo = pltpu.bitcast(xw << 16, F32).astype(BF16)
        hi = pltpu.bitcast(xw & jnp.uint32(0xFFFF0000), F32).astype(BF16)
        x = jnp.concatenate([lo, hi], axis=1)
        for r in range(rows):
            row_copy(ahead, ahead_slot, r).start()
        g = jnp.dot(x, wg_bf[...], preferred_element_type=F32)
        u = jnp.dot(x, wu_bf[...], preferred_element_type=F32)
        a = (g * _sigmoid(g) * u).astype(BF16)
        o_ref[...] = jnp.dot(a, wd_bf[...], preferred_element_type=F32)

    @pl.when(jnp.logical_not(active))
    def _():
        start_rows_loop(ahead, ahead_slot)
        o_ref[...] = jnp.zeros_like(o_ref)

    @pl.when(i == n - 1)
    def _():
        for s in range(EXPERT_AHEAD):
            wait_rows(n - 1, (n + s) % n_slot)


def moe_experts(order_tok, block_base, block_e, first, run, next_e, n_used, hn_packed, w_gate, w_up, w_down, layer,
                n_blocks):
    _, w = hn_packed.shape
    d = 2 * w
    de = w_gate.shape[3]
    return pl.pallas_call(
        functools.partial(_expert_kernel, layer=layer),
        grid_spec=pltpu.PrefetchScalarGridSpec(
            num_scalar_prefetch=7,
            grid=(n_blocks,),
            in_specs=[pl.BlockSpec(memory_space=pl.ANY)] * 4,
            out_specs=pl.BlockSpec((MOE_BLOCK, d), lambda i, *_: (i, 0)),
            scratch_shapes=[
                pltpu.VMEM((EXPERT_AHEAD + 1, MOE_BLOCK, w), jnp.uint32),
                pltpu.VMEM((2, d, de), F32), pltpu.VMEM((2, d, de), F32), pltpu.VMEM((2, de, d), F32),
                pltpu.VMEM((d, de), BF16), pltpu.VMEM((d, de), BF16), pltpu.VMEM((de, d), BF16),
                pltpu.SemaphoreType.DMA((2, 3)),
                pltpu.SemaphoreType.DMA((EXPERT_AHEAD + 1,)),
            ],
        ),
        out_shape=jax.ShapeDtypeStruct((n_blocks * MOE_BLOCK, d), F32),
        compiler_params=_cparams("arbitrary"),
        name="moe_experts",
    )(order_tok, block_base, block_e, first, run, next_e, n_used, hn_packed, w_gate, w_up, w_down)


def _combine_kernel(dest_ref, h_ref, gates_ref, p_ref, y_hbm, wpg_ref, wpp_ref, pn_ref, o_ref, ybuf, sems):
    st = h_ref.shape[0] // COMBINE_SUBS
    t = dest_ref.shape[0] // TOP_K
    i = pl.program_id(0)
    n_sub = pl.num_programs(0) * COMBINE_SUBS
    n_slot = ybuf.shape[0]

    def row_copy(sub, slot, r, k):
        d = dest_ref[k * t + sub * st + r]
        return pltpu.make_async_copy(y_hbm.at[pl.ds(d, 1)], ybuf.at[slot, k, pl.ds(r, 1)], sems.at[slot])

    def start_sub_loop(sub, slot):
        def body(r, c):
            for k in range(TOP_K):
                row_copy(sub, slot, r, k).start()
            return c
        lax.fori_loop(0, st, body, 0, unroll=DMA_UNROLL)

    def wait_sub(sub, slot):
        def body(r, c):
            for k in range(TOP_K):
                row_copy(sub, slot, r, k).wait()
            return c
        lax.fori_loop(0, st, body, 0, unroll=DMA_UNROLL)

    @pl.when(i == 0)
    def _():
        for s in range(COMBINE_AHEAD):
            start_sub_loop(s, s)

    for j in range(COMBINE_SUBS):
        sub = i * COMBINE_SUBS + j
        slot = sub % n_slot
        wait_sub(sub, slot)
        rows = slice(j * st, (j + 1) * st)
        gates = gates_ref[rows, :]
        h = h_ref[rows, :] + (gates[:, 0:1] * ybuf[slot, 0] + gates[:, 1:2] * ybuf[slot, 1])
        ahead = jnp.minimum(sub + COMBINE_AHEAD, n_sub - 1)
        ahead_slot = (sub + COMBINE_AHEAD) % n_slot
        for r in range(st):
            for k in range(TOP_K):
                row_copy(ahead, ahead_slot, r, k).start()
        gate = _sigmoid(jnp.dot(h.astype(BF16), wpg_ref[...], preferred_element_type=F32))
        e = jnp.dot(p_ref[rows, :].astype(BF16), wpp_ref[...], preferred_element_type=F32)
        o_ref[rows, :] = h + _rms(e * gate, pn_ref[...])

    @pl.when(i == pl.num_programs(0) - 1)
    def _():
        for s in range(COMBINE_AHEAD):
            wait_sub(n_sub - 1, (n_sub + s) % n_slot)


def moe_combine_ple(dest_flat, h, gates_tk, p, yb, w_ple_gate, w_ple_proj, ple_gain):
    t, d = h.shape
    tm = min(COMBINE_SUB * COMBINE_SUBS, t)
    sub = tm // COMBINE_SUBS
    pd = p.shape[1]
    row = lambda i, dest: (i, 0)
    const = lambda i, dest: (0, 0)
    resident = pl.Buffered(1)
    return pl.pallas_call(
        _combine_kernel,
        grid_spec=pltpu.PrefetchScalarGridSpec(
            num_scalar_prefetch=1,
            grid=(t // tm,),
            in_specs=[
                pl.BlockSpec((tm, d), row),
                pl.BlockSpec((tm, TOP_K), row),
                pl.BlockSpec((tm, pd), row),
                pl.BlockSpec(memory_space=pl.ANY),
                pl.BlockSpec((d, d), const, pipeline_mode=resident),
                pl.BlockSpec((pd, d), const, pipeline_mode=resident),
                pl.BlockSpec((1, d), const),
            ],
            out_specs=pl.BlockSpec((tm, d), row),
            scratch_shapes=[pltpu.VMEM((COMBINE_SUBS + COMBINE_AHEAD, TOP_K, sub, d), F32),
                            pltpu.SemaphoreType.DMA((COMBINE_SUBS + COMBINE_AHEAD,))],
        ),
        out_shape=jax.ShapeDtypeStruct((t, d), F32),
        compiler_params=_cparams("arbitrary"),
        name="moe_combine_ple",
    )(dest_flat, h, gates_tk, p, yb, w_ple_gate, w_ple_proj, ple_gain.reshape(1, d))


def _moe_ple_layer(h, p_i, norm_ffn, w_group, b_group, w_expert, b_expert, w_gate, w_up, w_down, layer,
                   ple_w_proj, ple_w_gate, ple_norm):
    t, d = h.shape
    ng, _, epg = w_expert.shape
    w_rt = jnp.concatenate([w_group.T, jnp.transpose(w_expert, (0, 2, 1)).reshape(ng * epg, d)], axis=0)
    b_r = jnp.concatenate([b_group, b_expert.reshape(-1)])
    pad = LANES - w_rt.shape[0]
    w_rt = jnp.pad(w_rt, ((0, pad), (0, 0)))
    b_r = jnp.broadcast_to(jnp.pad(b_r, (0, pad))[:, None], (LANES, LANES))

    hn_packed, eidx, gates, rank, counts = moe_router(h, norm_ffn, w_rt, b_r)

    counts = counts[:, 0]
    padded = (counts + MOE_BLOCK - 1) // MOE_BLOCK * MOE_BLOCK
    pad_end = jnp.cumsum(padded)
    pad_start = pad_end - padded
    n_blocks = (t * TOP_K + N_EXPERTS * (MOE_BLOCK - 1) + MOE_BLOCK - 1) // MOE_BLOCK
    expert_ids = jnp.arange(N_EXPERTS, dtype=jnp.int32)
    run_start = jnp.sum(jnp.where(eidx[..., None] == expert_ids, pad_start, 0), axis=-1)
    dest = (run_start + rank).reshape(-1).astype(jnp.int32)
    block_start = jnp.arange(n_blocks, dtype=jnp.int32) * MOE_BLOCK
    block_e = jnp.minimum(jnp.sum((pad_end[None, :] <= block_start[:, None]).astype(jnp.int32), axis=1),
                          N_EXPERTS - 1)
    first = jnp.concatenate([jnp.ones((1,), jnp.int32), (block_e[1:] != block_e[:-1]).astype(jnp.int32)])
    run = jnp.cumsum(first) - 1
    later_nonempty = (expert_ids[None, :] > expert_ids[:, None]) & (padded[None, :] > 0)
    next_of = jnp.min(jnp.where(later_nonempty, expert_ids[None, :], N_EXPERTS), axis=1)
    next_e = jnp.sum(jnp.where(block_e[:, None] == expert_ids, next_of, 0), axis=-1)
    n_used = (pad_end[-1:] // MOE_BLOCK).astype(jnp.int32)
    dense_start = jnp.cumsum(counts) - counts
    dense_pos = jnp.sum(jnp.where(eidx[..., None] == expert_ids, dense_start, 0), axis=-1) + rank
    order_tok = (jnp.argsort(dense_pos.reshape(-1)) % t).astype(jnp.int32)
    gap = pad_start - dense_start
    block_base = block_start - jnp.sum(jnp.where(block_e[:, None] == expert_ids, gap, 0), axis=-1)

    yb = moe_experts(order_tok, block_base.astype(jnp.int32), block_e, first, run.astype(jnp.int32),
                     next_e.astype(jnp.int32), n_used, hn_packed, w_gate, w_up, w_down, layer, n_blocks)
    return moe_combine_ple(dest, h, gates.T, p_i, yb, ple_w_gate.astype(BF16), ple_w_proj.astype(BF16), ple_norm)


def _lru_layer(h, batch, norm, w_in, conv_w, conv_b, w_rg, b_rg, w_ig, b_ig, lam, w_out):
    w_gates = jnp.concatenate([w_rg, w_ig], axis=-1).astype(BF16)
    return lru_core(h, norm, w_in.astype(BF16), conv_w, conv_b, w_gates, b_rg.reshape(-1), b_ig.reshape(-1), lam,
                    w_out.astype(BF16), batch=batch)


def _attn_layer(h, batch, cos, sin, norm, w_qkv, q_gain, k_gain, w_o):
    w_qkv = w_qkv.astype(BF16)
    outs, lses = [], []
    for g, (_, dilation) in enumerate(DILATION_PAIRS):
        qkv_g = qkv_project(h, norm, w_qkv, cos, sin, q_gain[g], k_gain[g], g, dilation, batch=batch)
        o, lse = dilated_attention(qkv_g)
        outs.append(o)
        lses.append(lse)
    return merge_project(outs, lses, h, w_o.astype(BF16), batch=batch)


def kernel(x, p, positions, norm_mix, norm_ffn, lru_w_in, lru_conv_w, lru_conv_b, lru_w_rg, lru_b_rg, lru_w_ig, lru_b_ig, lru_lambda, lru_w_out, attn_w_qkv, attn_q_gain, attn_k_gain, attn_w_o, moe_w_group, moe_b_group, moe_w_expert, moe_b_expert, moe_w_gate, moe_w_up, moe_w_down, ple_w_proj, ple_w_gate, ple_norm):
    batch, seq, d = x.shape
    depth = p.shape[0]
    t = batch * seq
    h = x.reshape(t, d)
    cos, sin = rope_tables(positions.reshape(t))
    for i in range(depth):
        j = i // 2
        if i % 2 == 0:
            h = _lru_layer(h, batch, norm_mix[i], lru_w_in[j], lru_conv_w[j], lru_conv_b[j], lru_w_rg[j],
                           lru_b_rg[j], lru_w_ig[j], lru_b_ig[j], lru_lambda[j], lru_w_out[j])
        else:
            h = _attn_layer(h, batch, cos, sin, norm_mix[i], attn_w_qkv[j], attn_q_gain[j], attn_k_gain[j],
                            attn_w_o[j])
        h = _moe_ple_layer(h, p[i].reshape(t, -1), norm_ffn[i], moe_w_group[i], moe_b_group[i],
                           moe_w_expert[i], moe_b_expert[i], moe_w_gate, moe_w_up, moe_w_down, i,
                           ple_w_proj[i], ple_w_gate[i], ple_norm[i])
    return h.reshape(batch, seq, d)
```

```python
import functools

import jax
import jax.numpy as jnp
from jax import lax
from jax.experimental import pallas as pl
from jax.experimental.pallas import tpu as pltpu

NORM_EPS = 1e-6
N_LRU_BLOCKS = 8
CONV_WIDTH = 4
LRU_C = 8.0
DILATION_PAIRS = ((128, 1), (512, 4), (2048, 16))
N_HEADS = 8
HEAD_DIM = 128
ATTN_BLOCK = 128
ROPE_THETA = 10000.0
NEG_INF = -1e30
N_GROUPS = 8
EXPERTS_PER_GROUP = 8
N_EXPERTS = N_GROUPS * EXPERTS_PER_GROUP
TOP_K = 2

LANES = 128
SUBLANES = 8
VMEM_LIMIT_BYTES = 56 * 1024 * 1024

ROW_TILE = 512
LRU_CHUNK = 256
ATTN_TILE = 512
MOE_BLOCK = 256
EXPERT_AHEAD = 3
N_DMA_QUEUES = 2
WEIGHT_DMA_PRIORITY = 1
QKV_TILE = 256
QKV_CHUNK = 256
MERGE_TILE = 256
DMA_UNROLL = 8
COMBINE_SUB = 256
COMBINE_SUBS = 2
COMBINE_AHEAD = 2

F32 = jnp.float32
BF16 = jnp.bfloat16


def _cparams(*sem):
    return pltpu.CompilerParams(dimension_semantics=sem, vmem_limit_bytes=VMEM_LIMIT_BYTES)


def _rms(x, gain):
    ms = jnp.mean(x * x, axis=-1, keepdims=True)
    return x * lax.rsqrt(ms + NORM_EPS) * gain


def _sigmoid(x):
    return 0.5 * jnp.tanh(0.5 * x) + 0.5


def _shift_rows_in_vreg(x, s, fill):
    g = x.shape[0] // SUBLANES
    x3 = x.reshape(g, SUBLANES, x.shape[1])
    rolled = pltpu.roll(x3, s, axis=1)
    row = lax.broadcasted_iota(jnp.int32, x3.shape, 1)
    return jnp.where(row >= s, rolled, fill).reshape(x.shape)


def _shift_rows(x, s, fill):
    row = lax.broadcasted_iota(jnp.int32, x.shape, 0)
    return jnp.where(row >= s, pltpu.roll(x, s, axis=0), fill)


def _one_minus_exp(x, ex):
    near = jnp.where(ex == 1.0, -x, (1.0 - ex) * x / jnp.log(ex))
    return jnp.where(x > -0.5, near, 1.0 - ex)


def _lru_kernel(h_ref, gmix_ref, win_ref, cw_ref, cb_ref, wg_ref, brg_ref, big_ref, lam_ref, wo_ref,
                o_ref, xn_ref, xprev_ref, hcarry_ref, xc_ref, a_ref, u_ref, hin_ref, tot_ref):
    tc, d = h_ref.shape
    blk = d // N_LRU_BLOCKS

    @pl.when(pl.program_id(1) == 0)
    def _():
        xprev_ref[...] = jnp.zeros_like(xprev_ref)
        hcarry_ref[...] = jnp.zeros_like(hcarry_ref)

    xn_ref[...] = _rms(h_ref[...], gmix_ref[...]).astype(BF16)
    x = jnp.dot(xn_ref[...], win_ref[:, d:], preferred_element_type=F32)

    prev = xprev_ref[...]
    cw = cw_ref[...]
    acc = cb_ref[...] + cw[CONV_WIDTH - 1:CONV_WIDTH] * x
    head = cb_ref[...] + cw[CONV_WIDTH - 1:CONV_WIDTH] * x[0:SUBLANES]
    row8 = lax.broadcasted_iota(jnp.int32, (SUBLANES, d), 0)
    for s in range(1, CONV_WIDTH):
        wk = cw[CONV_WIDTH - 1 - s:CONV_WIDTH - s]
        acc = acc + wk * pltpu.roll(x, s, axis=0)
        first = jnp.where(row8 >= s, pltpu.roll(x[0:SUBLANES], s, axis=0), pltpu.roll(prev, s, axis=0))
        head = head + wk * first
    xc_ref[...] = acc
    xc_ref[0:SUBLANES, :] = head
    xprev_ref[...] = x[tc - SUBLANES:tc]

    lam = lam_ref[...]
    log_sig = jnp.minimum(lam, 0.0) - jnp.log1p(jnp.exp(-jnp.abs(lam)))

    for n in range(N_LRU_BLOCKS):
        sl = slice(n * blk, (n + 1) * blk)
        xc = xc_ref[:, sl]
        z = jnp.dot(xc.astype(BF16), wg_ref[n], preferred_element_type=F32)
        r = _sigmoid(z[:, :blk] + brg_ref[:, sl])
        i = _sigmoid(z[:, blk:] + big_ref[:, sl])
        log_a = LRU_C * r * log_sig[:, sl]
        a = jnp.exp(log_a)
        a_ref[:, sl] = a
        u_ref[:, sl] = jnp.sqrt(_one_minus_exp(2.0 * log_a, a * a)) * (i * xc)

    a = a_ref[...]
    u = u_ref[...]
    for s in (1, 2, 4):
        a_sh = _shift_rows_in_vreg(a, s, 1.0)
        u_sh = _shift_rows_in_vreg(u, s, 0.0)
        u = a * u_sh + u
        a = a * a_sh
    a_ref[...] = a
    u_ref[...] = u
    ng = tc // SUBLANES
    for j in range(ng):
        last = slice(j * SUBLANES + SUBLANES - 1, (j + 1) * SUBLANES)
        hin_ref[j:j + 1, :] = a_ref[last, :]
        tot_ref[j:j + 1, :] = u_ref[last, :]
    pa = hin_ref[...]
    su = tot_ref[...]
    s = 1
    while s < ng:
        su = pa * _shift_rows(su, s, 0.0) + su
        pa = pa * _shift_rows(pa, s, 1.0)
        s *= 2
    h0 = hcarry_ref[...]
    after = pa * h0 + su
    hcarry_ref[...] = after[ng - 1:ng]
    row = lax.broadcasted_iota(jnp.int32, after.shape, 0)
    hin_ref[...] = jnp.where(row == 0, h0, pltpu.roll(after, 1, axis=0))
    for j in range(ng):
        rows = slice(j * SUBLANES, (j + 1) * SUBLANES)
        h_in = jnp.broadcast_to(hin_ref[j:j + 1, :], (SUBLANES, d))
        u_ref[rows, :] = a_ref[rows, :] * h_in + u_ref[rows, :]

    gate = jnp.dot(xn_ref[...], win_ref[:, :d], preferred_element_type=F32)
    y = jax.nn.gelu(gate, approximate=True) * u_ref[...]
    o_ref[...] = h_ref[...] + jnp.dot(y.astype(BF16), wo_ref[...], preferred_element_type=F32)


def lru_core(h, gain, w_in, conv_w, conv_b, w_gates, b_rg, b_ig, lam, w_out, *, batch):
    t, d = h.shape
    seq = t // batch
    tc = min(LRU_CHUNK, seq)
    nc = seq // tc
    blk = d // N_LRU_BLOCKS
    row = lambda b, c: (b * nc + c, 0)
    const2 = lambda b, c: (0, 0)
    resident = pl.Buffered(1)
    return pl.pallas_call(
        _lru_kernel,
        grid=(batch, nc),
        in_specs=[
            pl.BlockSpec((tc, d), row),
            pl.BlockSpec((1, d), const2),
            pl.BlockSpec((d, 2 * d), const2, pipeline_mode=resident),
            pl.BlockSpec((CONV_WIDTH, d), const2),
            pl.BlockSpec((1, d), const2),
            pl.BlockSpec((N_LRU_BLOCKS, blk, 2 * blk), lambda b, c: (0, 0, 0), pipeline_mode=resident),
            pl.BlockSpec((1, d), const2),
            pl.BlockSpec((1, d), const2),
            pl.BlockSpec((1, d), const2),
            pl.BlockSpec((d, d), const2, pipeline_mode=resident),
        ],
        out_specs=pl.BlockSpec((tc, d), row),
        out_shape=jax.ShapeDtypeStruct((t, d), F32),
        scratch_shapes=[
            pltpu.VMEM((tc, d), BF16),
            pltpu.VMEM((SUBLANES, d), F32),
            pltpu.VMEM((1, d), F32),
            pltpu.VMEM((tc, d), F32),
            pltpu.VMEM((tc, d), F32),
            pltpu.VMEM((tc, d), F32),
            pltpu.VMEM((tc // SUBLANES, d), F32),
            pltpu.VMEM((tc // SUBLANES, d), F32),
        ],
        compiler_params=_cparams("parallel", "arbitrary"),
        name="lru_core",
    )(h, gain.reshape(1, d), w_in, conv_w, conv_b.reshape(1, d), w_gates, b_rg.reshape(1, d), b_ig.reshape(1, d),
      lam.reshape(1, d), w_out)


def _rope_kernel(pos_ref, inv_ref, cos_ref, sin_ref):
    ang = pos_ref[...].astype(F32) * inv_ref[...]
    lane = lax.broadcasted_iota(jnp.int32, ang.shape, 1)
    cos_ref[...] = jnp.cos(ang)
    sin_ref[...] = jnp.where(lane < HEAD_DIM // 2, -jnp.sin(ang), jnp.sin(ang))


def rope_tables(positions_flat):
    t = positions_flat.shape[0]
    tm = min(ROW_TILE, t)
    inv = ROPE_THETA ** (-jnp.arange(0, HEAD_DIM, 2, dtype=F32) / HEAD_DIM)
    inv2 = jnp.concatenate([inv, inv]).reshape(1, HEAD_DIM)
    return pl.pallas_call(
        _rope_kernel,
        grid=(t // tm,),
        in_specs=[pl.BlockSpec((tm, 1), lambda i: (i, 0)), pl.BlockSpec((1, HEAD_DIM), lambda i: (0, 0))],
        out_specs=[pl.BlockSpec((tm, HEAD_DIM), lambda i: (i, 0))] * 2,
        out_shape=[jax.ShapeDtypeStruct((t, HEAD_DIM), F32)] * 2,
        compiler_params=_cparams("parallel"),
        name="rope_tables",
    )(positions_flat.reshape(t, 1), inv2)


def _qkv_kernel(x_ref, g_ref, w_ref, cos_ref, sin_ref, qg_ref, kg_ref, o_ref, stage_ref, stage2_ref, *, dilation):
    assert dilation in (1, 4, 16)
    tm = x_ref.shape[0]
    hd = N_HEADS * HEAD_DIM
    xn = _rms(x_ref[...], g_ref[...]).astype(BF16)
    cos = cos_ref[...]
    sin = sin_ref[...]
    for kind in range(3):
        for c in range(hd // QKV_CHUNK):
            cols = slice(kind * hd + c * QKV_CHUNK, kind * hd + (c + 1) * QKV_CHUNK)
            z = jnp.dot(xn, w_ref[:, cols], preferred_element_type=F32)
            if kind < 2:
                gain = (qg_ref if kind == 0 else kg_ref)[...]
                heads = []
                for hh in range(QKV_CHUNK // HEAD_DIM):
                    y = _rms(z[:, hh * HEAD_DIM:(hh + 1) * HEAD_DIM], gain)
                    y = y * cos + pltpu.roll(y, HEAD_DIM // 2, axis=1) * sin
                    heads.append(y * (HEAD_DIM ** -0.5) if kind == 0 else y)
                z = jnp.concatenate(heads, axis=1)
            if dilation == 1:
                o_ref[0, :, cols] = z.astype(BF16)
            else:
                for hh in range(QKV_CHUNK // LANES):
                    stage_ref[cols.start // LANES + hh] = z[:, hh * LANES:(hh + 1) * LANES]
    if dilation == 4:
        for r in range(dilation):
            for cc in range(3 * hd // LANES):
                o_ref[r, :, cc * LANES:(cc + 1) * LANES] = (
                    stage_ref[cc, pl.ds(r, tm // dilation, stride=dilation), :].astype(BF16))
    if dilation == 16:
        quarter = tm // 4
        for cc in range(3 * hd // LANES):
            for r1 in range(4):
                stage2_ref[cc, r1 * quarter:(r1 + 1) * quarter, :] = stage_ref[cc, pl.ds(r1, quarter, stride=4), :]
            for r1 in range(4):
                for r2 in range(4):
                    o_ref[r1 + 4 * r2, :, cc * LANES:(cc + 1) * LANES] = (
                        stage2_ref[cc, pl.ds(r1 * quarter + r2, tm // dilation, stride=4), :].astype(BF16))


def qkv_project(x, gain, w_qkv, cos, sin, q_gain, k_gain, group, dilation, *, batch):
    t, k = x.shape
    seq = t // batch
    sub = seq // dilation
    hd3 = 3 * N_HEADS * HEAD_DIM
    tm = min(QKV_TILE, seq)
    per_batch = seq // tm
    row = lambda i: (i, 0)
    const = lambda i: (0, 0)
    return pl.pallas_call(
        functools.partial(_qkv_kernel, dilation=dilation),
        grid=(t // tm,),
        in_specs=[
            pl.BlockSpec((tm, k), row),
            pl.BlockSpec((1, k), const),
            pl.BlockSpec((k, hd3), lambda i: (0, group)),
            pl.BlockSpec((tm, HEAD_DIM), row),
            pl.BlockSpec((tm, HEAD_DIM), row),
            pl.BlockSpec((1, HEAD_DIM), const),
            pl.BlockSpec((1, HEAD_DIM), const),
        ],
        out_specs=pl.BlockSpec((None, dilation, tm // dilation, hd3),
                               lambda i: (i // per_batch, 0, i % per_batch, 0)),
        out_shape=jax.ShapeDtypeStruct((batch, dilation, sub, hd3), BF16),
        scratch_shapes=[pltpu.VMEM((hd3 // LANES, tm if dilation > 1 else SUBLANES, LANES), F32),
                        pltpu.VMEM((hd3 // LANES, tm if dilation == 16 else SUBLANES, LANES), F32)],
        compiler_params=_cparams("parallel"),
        name=f"qkv_project_d{dilation}",
    )(x, gain.reshape(1, k), w_qkv, cos, sin, q_gain.reshape(1, HEAD_DIM), k_gain.reshape(1, HEAD_DIM))


def _attn_kernel(q_ref, kc_ref, kp_ref, vc_ref, vp_ref, o_ref, lse_ref, kbuf_ref, vbuf_ref):
    tl = q_ref.shape[0]
    blk = ATTN_BLOCK
    first_tile = pl.program_id(2) == 0

    kbuf_ref[0:blk, :] = kp_ref[...]
    kbuf_ref[blk:, :] = kc_ref[...]
    vbuf_ref[0:blk, :] = vp_ref[...]
    vbuf_ref[blk:, :] = vc_ref[...]

    qi = lax.broadcasted_iota(jnp.int32, (blk, 2 * blk), 0)
    kj = lax.broadcasted_iota(jnp.int32, (blk, 2 * blk), 1)
    upper = kj <= qi + blk
    bias_gen = jnp.where(upper & (kj >= qi), 0.0, NEG_INF).astype(F32)
    bias_first = jnp.where(upper & (kj >= blk), 0.0, NEG_INF).astype(F32)
    bias0 = jnp.where(first_tile, bias_first, bias_gen)
    lane = lax.broadcasted_iota(jnp.int32, (blk, LANES), 1)

    for qb in range(tl // blk):
        bias = bias0 if qb == 0 else bias_gen
        rows = slice(qb * blk, (qb + 1) * blk)
        win = slice(qb * blk, (qb + 2) * blk)
        lse_tile = jnp.zeros((blk, LANES), F32)
        for h in range(N_HEADS):
            cols = slice(h * HEAD_DIM, (h + 1) * HEAD_DIM)
            s = lax.dot_general(q_ref[rows, cols], kbuf_ref[win, cols], (((1,), (1,)), ((), ())),
                                preferred_element_type=F32) + bias
            m = jnp.max(s, axis=-1, keepdims=True)
            p = jnp.exp(s - m)
            den = jnp.sum(p, axis=-1, keepdims=True)
            o = jnp.dot(p.astype(BF16), vbuf_ref[win, cols], preferred_element_type=F32)
            o_ref[rows, cols] = o / den
            lse_tile = jnp.where(lane == h, m + jnp.log(den), lse_tile)
        lse_ref[rows, :] = lse_tile


def dilated_attention(qkv_g):
    batch, dilation, sub, hd3 = qkv_g.shape
    hd = hd3 // 3
    tl = min(ATTN_TILE, sub)
    per = tl // ATTN_BLOCK
    cur = lambda kind: (lambda b, r, i: (b, r, i, kind))
    prev = lambda kind: (lambda b, r, i: (b, r, jnp.maximum(i * per - 1, 0), kind))
    return pl.pallas_call(
        _attn_kernel,
        grid=(batch, dilation, sub // tl),
        in_specs=[
            pl.BlockSpec((None, None, tl, hd), cur(0)),
            pl.BlockSpec((None, None, tl, hd), cur(1)),
            pl.BlockSpec((None, None, ATTN_BLOCK, hd), prev(1)),
            pl.BlockSpec((None, None, tl, hd), cur(2)),
            pl.BlockSpec((None, None, ATTN_BLOCK, hd), prev(2)),
        ],
        out_specs=[
            pl.BlockSpec((None, None, tl, hd), lambda b, r, i: (b, r, i, 0)),
            pl.BlockSpec((None, None, tl, LANES), lambda b, r, i: (b, r, i, 0)),
        ],
        out_shape=[
            jax.ShapeDtypeStruct((batch, dilation, sub, hd), F32),
            jax.ShapeDtypeStruct((batch, dilation, sub, LANES), F32),
        ],
        scratch_shapes=[pltpu.VMEM((tl + ATTN_BLOCK, hd), BF16), pltpu.VMEM((tl + ATTN_BLOCK, hd), BF16)],
        compiler_params=_cparams("parallel", "parallel", "arbitrary"),
        name=f"dilated_attention_d{dilation}",
    )(qkv_g, qkv_g, qkv_g, qkv_g, qkv_g)


def _interleave(src_ref, dst_ref, tmp_ref):
    dilation, rows, width = src_ref.shape
    assert dilation in (1, 4, 16)
    for c in range(width // LANES):
        cols = slice(c * LANES, (c + 1) * LANES)
        if dilation == 1:
            dst_ref[c] = src_ref[0, :, cols]
        elif dilation == 4:
            for r in range(dilation):
                dst_ref[c, pl.ds(r, rows, stride=dilation), :] = src_ref[r, :, cols]
        else:
            quarter = 4 * rows
            for r1 in range(4):
                for r2 in range(4):
                    tmp_ref[pl.ds(r1 * quarter + r2, rows, stride=4), :] = src_ref[r1 + 4 * r2, :, cols]
            for r1 in range(4):
                dst_ref[c, pl.ds(r1, quarter, stride=4), :] = tmp_ref[r1 * quarter:(r1 + 1) * quarter, :]


def _merge_kernel(o0_ref, o1_ref, o2_ref, l0_ref, l1_ref, l2_ref, hres_ref, wo_ref, out_ref, obuf, lbuf, tmp):
    for g, (o_ref, l_ref) in enumerate(((o0_ref, l0_ref), (o1_ref, l1_ref), (o2_ref, l2_ref))):
        _interleave(o_ref, obuf.at[g], tmp)
        _interleave(l_ref, lbuf.at[g], tmp)
    l0, l1, l2 = lbuf[0, 0], lbuf[1, 0], lbuf[2, 0]
    m = jnp.maximum(jnp.maximum(l0, l1), l2)
    e0, e1, e2 = jnp.exp(l0 - m), jnp.exp(l1 - m), jnp.exp(l2 - m)
    tot = e0 + e1 + e2
    w0, w1, w2 = e0 / tot, e1 / tot, e2 / tot
    tm = l0.shape[0]
    parts = []
    for h in range(N_HEADS):
        bc = lambda w: jnp.broadcast_to(w[:, h:h + 1], (tm, HEAD_DIM))
        parts.append(bc(w0) * obuf[0, h] + bc(w1) * obuf[1, h] + bc(w2) * obuf[2, h])
    o = jnp.concatenate(parts, axis=1).astype(BF16)
    out_ref[...] = hres_ref[...] + jnp.dot(o, wo_ref[...], preferred_element_type=F32)


def merge_project(outs, lses, h_res, w_o, *, batch):
    t, d = h_res.shape
    seq = t // batch
    hd = N_HEADS * HEAD_DIM
    tm = min(MERGE_TILE, seq)
    per_batch = seq // tm
    row = lambda i: (i, 0)

    def grouped(a):
        dil, width = a.shape[1], a.shape[3]
        return pl.BlockSpec((None, dil, tm // dil, width), lambda i: (i // per_batch, 0, i % per_batch, 0))

    return pl.pallas_call(
        _merge_kernel,
        grid=(t // tm,),
        in_specs=[grouped(a) for a in outs] + [grouped(a) for a in lses] + [
            pl.BlockSpec((tm, d), row),
            pl.BlockSpec((hd, d), lambda i: (0, 0)),
        ],
        out_specs=pl.BlockSpec((tm, d), row),
        out_shape=jax.ShapeDtypeStruct((t, d), F32),
        scratch_shapes=[pltpu.VMEM((len(outs), hd // LANES, tm, LANES), F32),
                        pltpu.VMEM((len(lses), 1, tm, LANES), F32),
                        pltpu.VMEM((tm, LANES), F32)],
        compiler_params=_cparams("parallel"),
        name="merge_project",
    )(*outs, *lses, h_res, w_o)


def _router_kernel(h_ref, g_ref, wr_ref, br_ref, hn_ref, eidx_ref, gate_ref, rank_ref, cnt_ref, carry_ref):
    tm = h_ref.shape[0]
    half = h_ref.shape[1] // 2

    @pl.when(pl.program_id(0) == 0)
    def _():
        carry_ref[...] = jnp.zeros_like(carry_ref)

    hn = _rms(h_ref[...], g_ref[...])
    lo = pltpu.bitcast(hn[:, :half].astype(BF16).astype(F32), jnp.uint32)
    hi = pltpu.bitcast(hn[:, half:].astype(BF16).astype(F32), jnp.uint32)
    hn_ref[...] = hi | (lo >> 16)

    logits = lax.dot_general(wr_ref[...], hn, (((1,), (1,)), ((), ())), precision=lax.Precision.HIGHEST,
                             preferred_element_type=F32) + br_ref[:, 0:1]
    lg = logits[0:N_GROUPS]
    gi = lax.broadcasted_iota(jnp.int32, (N_GROUPS, tm), 0)
    m = jnp.max(lg, axis=0, keepdims=True)
    p_top = 1.0 / jnp.sum(jnp.exp(lg - m), axis=0, keepdims=True)
    g_top = jnp.min(jnp.where(lg == m, gi, N_GROUPS), axis=0, keepdims=True)

    le = jnp.zeros((EXPERTS_PER_GROUP, tm), F32)
    for g in range(N_GROUPS):
        lo_row = N_GROUPS + g * EXPERTS_PER_GROUP
        le = jnp.where(g_top == g, logits[lo_row:lo_row + EXPERTS_PER_GROUP], le)
    ei = lax.broadcasted_iota(jnp.int32, (EXPERTS_PER_GROUP, tm), 0)
    v1 = jnp.max(le, axis=0, keepdims=True)
    i1 = jnp.min(jnp.where(le == v1, ei, EXPERTS_PER_GROUP), axis=0, keepdims=True)
    le2 = jnp.where(ei == i1, -jnp.inf, le)
    v2 = jnp.max(le2, axis=0, keepdims=True)
    i2 = jnp.min(jnp.where(le2 == v2, ei, EXPERTS_PER_GROUP), axis=0, keepdims=True)
    e21 = jnp.exp(v2 - v1)
    gate1 = p_top / (1.0 + e21)
    gate2 = p_top * e21 / (1.0 + e21)
    e1 = g_top * EXPERTS_PER_GROUP + i1
    e2 = g_top * EXPERTS_PER_GROUP + i2
    eidx_ref[...] = jnp.concatenate([e1, e2], axis=0)
    gate_ref[...] = jnp.concatenate([gate1, gate2], axis=0)

    ee = lax.broadcasted_iota(jnp.int32, (N_EXPERTS, tm), 0)
    oh1 = ee == e1
    oh2 = ee == e2
    onehot = jnp.where(oh1 | oh2, 1.0, 0.0)
    tri = jnp.where(lax.broadcasted_iota(jnp.int32, (tm, tm), 0) < lax.broadcasted_iota(jnp.int32, (tm, tm), 1),
                    1.0, 0.0).astype(BF16)
    before = jnp.dot(onehot.astype(BF16), tri, preferred_element_type=F32) + carry_ref[:, 0:1]
    r1 = jnp.sum(jnp.where(oh1, before, 0.0), axis=0, keepdims=True)
    r2 = jnp.sum(jnp.where(oh2, before, 0.0), axis=0, keepdims=True)
    rank_ref[...] = jnp.concatenate([r1, r2], axis=0).astype(jnp.int32)
    carry_ref[...] = carry_ref[...] + jnp.sum(onehot, axis=1, keepdims=True)
    cnt_ref[...] = carry_ref[...].astype(jnp.int32)


def moe_router(h, gain, w_router_t, b_router):
    t, d = h.shape
    tm = min(ROW_TILE, t)
    nlog = w_router_t.shape[0]
    return pl.pallas_call(
        _router_kernel,
        grid=(t // tm,),
        in_specs=[
            pl.BlockSpec((tm, d), lambda i: (i, 0)),
            pl.BlockSpec((1, d), lambda i: (0, 0)),
            pl.BlockSpec((nlog, d), lambda i: (0, 0)),
            pl.BlockSpec((nlog, LANES), lambda i: (0, 0)),
        ],
        out_specs=[
            pl.BlockSpec((tm, d // 2), lambda i: (i, 0)),
            pl.BlockSpec((TOP_K, tm), lambda i: (0, i)),
            pl.BlockSpec((TOP_K, tm), lambda i: (0, i)),
            pl.BlockSpec((TOP_K, tm), lambda i: (0, i)),
            pl.BlockSpec((N_EXPERTS, LANES), lambda i: (0, 0)),
        ],
        out_shape=[
            jax.ShapeDtypeStruct((t, d // 2), jnp.uint32),
            jax.ShapeDtypeStruct((TOP_K, t), jnp.int32),
            jax.ShapeDtypeStruct((TOP_K, t), F32),
            jax.ShapeDtypeStruct((TOP_K, t), jnp.int32),
            jax.ShapeDtypeStruct((N_EXPERTS, LANES), jnp.int32),
        ],
        scratch_shapes=[pltpu.VMEM((N_EXPERTS, LANES), F32)],
        compiler_params=_cparams("arbitrary"),
        name="moe_router",
    )(h, gain.reshape(1, d), w_router_t, b_router)


def _expert_kernel(tok_ref, base_ref, be_ref, first_ref, run_ref, next_ref, used_ref,
                   hn_hbm, wg_hbm, wu_hbm, wd_hbm, o_ref,
                   xbuf, wg_f32, wu_f32, wd_f32, wg_bf, wu_bf, wd_bf, sems, xsems, *, layer):
    i = pl.program_id(0)
    n = pl.num_programs(0)
    rows = xbuf.shape[1]
    n_slot = xbuf.shape[0]
    last_tok = tok_ref.shape[0] - 1
    active = i < used_ref[0]

    def row_copy(blk, slot, r):
        tok = tok_ref[jnp.minimum(base_ref[blk] + r, last_tok)]
        return pltpu.make_async_copy(hn_hbm.at[pl.ds(tok, 1)], xbuf.at[slot, pl.ds(r, 1)], xsems.at[slot])

    def start_rows_loop(blk, slot):
        def body(r, c):
            row_copy(blk, slot, r).start()
            return c
        lax.fori_loop(0, rows, body, 0, unroll=DMA_UNROLL)

    def wait_rows(blk, slot):
        def body(r, c):
            row_copy(blk, slot, r).wait()
            return c
        lax.fori_loop(0, rows, body, 0, unroll=DMA_UNROLL)

    def weight_copies(e, slot):
        return (pltpu.make_async_copy(wg_hbm.at[layer, e], wg_f32.at[slot], sems.at[slot, 0]),
                pltpu.make_async_copy(wu_hbm.at[layer, e], wu_f32.at[slot], sems.at[slot, 1]),
                pltpu.make_async_copy(wd_hbm.at[layer, e], wd_f32.at[slot], sems.at[slot, 2]))

    ahead = jnp.minimum(i + EXPERT_AHEAD, n - 1)
    ahead_slot = (i + EXPERT_AHEAD) % n_slot
    slot_x = i % n_slot

    @pl.when(i == 0)
    def _():
        for c in weight_copies(be_ref[0], 0):
            c.start(priority=WEIGHT_DMA_PRIORITY)
        for s in range(EXPERT_AHEAD):
            start_rows_loop(s, s)

    @pl.when(jnp.logical_and(active, first_ref[i] == 1))
    def _():
        slot = run_ref[i] % 2
        nxt = next_ref[i]

        @pl.when(nxt < N_EXPERTS)
        def _():
            for c in weight_copies(nxt, 1 - slot):
                c.start(priority=WEIGHT_DMA_PRIORITY)

        for c in weight_copies(be_ref[i], slot):
            c.wait()
        wg_bf[...] = wg_f32[slot].astype(BF16)
        wu_bf[...] = wu_f32[slot].astype(BF16)
        wd_bf[...] = wd_f32[slot].astype(BF16)

    wait_rows(i, slot_x)

    @pl.when(active)
    def _():
        xw = xbuf[slot_x]
        lo = pltpu.bitcast(xw << 16, F32).astype(BF16)
        hi = pltpu.bitcast(xw & jnp.uint32(0xFFFF0000), F32).astype(BF16)
        x = jnp.concatenate([lo, hi], axis=1)
        for r in range(rows):
            row_copy(ahead, ahead_slot, r).start(priority=r % N_DMA_QUEUES)
        g = jnp.dot(x, wg_bf[...], preferred_element_type=F32)
        u = jnp.dot(x, wu_bf[...], preferred_element_type=F32)
        a = (g * _sigmoid(g) * u).astype(BF16)
        o_ref[...] = jnp.dot(a, wd_bf[...], preferred_element_type=F32)

    @pl.when(jnp.logical_not(active))
    def _():
        start_rows_loop(ahead, ahead_slot)
        o_ref[...] = jnp.zeros_like(o_ref)

    @pl.when(i == n - 1)
    def _():
        for s in range(EXPERT_AHEAD):
            wait_rows(n - 1, (n + s) % n_slot)


def moe_experts(order_tok, block_base, block_e, first, run, next_e, n_used, hn_packed, w_gate, w_up, w_down, layer,
                n_blocks):
    _, w = hn_packed.shape
    d = 2 * w
    de = w_gate.shape[3]
    return pl.pallas_call(
        functools.partial(_expert_kernel, layer=layer),
        grid_spec=pltpu.PrefetchScalarGridSpec(
            num_scalar_prefetch=7,
            grid=(n_blocks,),
            in_specs=[pl.BlockSpec(memory_space=pl.ANY)] * 4,
            out_specs=pl.BlockSpec((MOE_BLOCK, d), lambda i, *_: (i, 0)),
            scratch_shapes=[
                pltpu.VMEM((EXPERT_AHEAD + 1, MOE_BLOCK, w), jnp.uint32),
                pltpu.VMEM((2, d, de), F32), pltpu.VMEM((2, d, de), F32), pltpu.VMEM((2, de, d), F32),
                pltpu.VMEM((d, de), BF16), pltpu.VMEM((d, de), BF16), pltpu.VMEM((de, d), BF16),
                pltpu.SemaphoreType.DMA((2, 3)),
                pltpu.SemaphoreType.DMA((EXPERT_AHEAD + 1,)),
            ],
        ),
        out_shape=jax.ShapeDtypeStruct((n_blocks * MOE_BLOCK, d), F32),
        compiler_params=_cparams("arbitrary"),
        name="moe_experts",
    )(order_tok, block_base, block_e, first, run, next_e, n_used, hn_packed, w_gate, w_up, w_down)


def _combine_kernel(dest_ref, h_ref, gates_ref, p_ref, y_hbm, wpg_ref, wpp_ref, pn_ref, o_ref, ybuf, sems):
    st = h_ref.shape[0] // COMBINE_SUBS
    t = dest_ref.shape[0] // TOP_K
    i = pl.program_id(0)
    n_sub = pl.num_programs(0) * COMBINE_SUBS
    n_slot = ybuf.shape[0]

    def row_copy(sub, slot, r, k):
        d = dest_ref[k * t + sub * st + r]
        return pltpu.make_async_copy(y_hbm.at[pl.ds(d, 1)], ybuf.at[slot, k, pl.ds(r, 1)], sems.at[slot])

    def start_sub_loop(sub, slot):
        def body(r, c):
            for k in range(TOP_K):
                row_copy(sub, slot, r, k).start()
            return c
        lax.fori_loop(0, st, body, 0, unroll=DMA_UNROLL)

    def wait_sub(sub, slot):
        def body(r, c):
            for k in range(TOP_K):
                row_copy(sub, slot, r, k).wait()
            return c
        lax.fori_loop(0, st, body, 0, unroll=DMA_UNROLL)

    @pl.when(i == 0)
    def _():
        for s in range(COMBINE_AHEAD):
            start_sub_loop(s, s)

    for j in range(COMBINE_SUBS):
        sub = i * COMBINE_SUBS + j
        slot = sub % n_slot
        wait_sub(sub, slot)
        rows = slice(j * st, (j + 1) * st)
        gates = gates_ref[rows, :]
        h = h_ref[rows, :] + (gates[:, 0:1] * ybuf[slot, 0] + gates[:, 1:2] * ybuf[slot, 1])
        ahead = jnp.minimum(sub + COMBINE_AHEAD, n_sub - 1)
        ahead_slot = (sub + COMBINE_AHEAD) % n_slot
        for r in range(st):
            for k in range(TOP_K):
                row_copy(ahead, ahead_slot, r, k).start(priority=k % N_DMA_QUEUES)
        gate = _sigmoid(jnp.dot(h.astype(BF16), wpg_ref[...], preferred_element_type=F32))
        e = jnp.dot(p_ref[rows, :].astype(BF16), wpp_ref[...], preferred_element_type=F32)
        o_ref[rows, :] = h + _rms(e * gate, pn_ref[...])

    @pl.when(i == pl.num_programs(0) - 1)
    def _():
        for s in range(COMBINE_AHEAD):
            wait_sub(n_sub - 1, (n_sub + s) % n_slot)


def moe_combine_ple(dest_flat, h, gates_tk, p, yb, w_ple_gate, w_ple_proj, ple_gain):
    t, d = h.shape
    tm = min(COMBINE_SUB * COMBINE_SUBS, t)
    sub = tm // COMBINE_SUBS
    pd = p.shape[1]
    row = lambda i, dest: (i, 0)
    const = lambda i, dest: (0, 0)
    resident = pl.Buffered(1)
    return pl.pallas_call(
        _combine_kernel,
        grid_spec=pltpu.PrefetchScalarGridSpec(
            num_scalar_prefetch=1,
            grid=(t // tm,),
            in_specs=[
                pl.BlockSpec((tm, d), row),
                pl.BlockSpec((tm, TOP_K), row),
                pl.BlockSpec((tm, pd), row),
                pl.BlockSpec(memory_space=pl.ANY),
                pl.BlockSpec((d, d), const, pipeline_mode=resident),
                pl.BlockSpec((pd, d), const, pipeline_mode=resident),
                pl.BlockSpec((1, d), const),
            ],
            out_specs=pl.BlockSpec((tm, d), row),
            scratch_shapes=[pltpu.VMEM((COMBINE_SUBS + COMBINE_AHEAD, TOP_K, sub, d), F32),
                            pltpu.SemaphoreType.DMA((COMBINE_SUBS + COMBINE_AHEAD,))],
        ),
        out_shape=jax.ShapeDtypeStruct((t, d), F32),
        compiler_params=_cparams("arbitrary"),
        name="moe_combine_ple",
    )(dest_flat, h, gates_tk, p, yb, w_ple_gate, w_ple_proj, ple_gain.reshape(1, d))


def _moe_ple_layer(h, p_i, norm_ffn, w_group, b_group, w_expert, b_expert, w_gate, w_up, w_down, layer,
                   ple_w_proj, ple_w_gate, ple_norm):
    t, d = h.shape
    ng, _, epg = w_expert.shape
    w_rt = jnp.concatenate([w_group.T, jnp.transpose(w_expert, (0, 2, 1)).reshape(ng * epg, d)], axis=0)
    b_r = jnp.concatenate([b_group, b_expert.reshape(-1)])
    pad = LANES - w_rt.shape[0]
    w_rt = jnp.pad(w_rt, ((0, pad), (0, 0)))
    b_r = jnp.broadcast_to(jnp.pad(b_r, (0, pad))[:, None], (LANES, LANES))

    hn_packed, eidx, gates, rank, counts = moe_router(h, norm_ffn, w_rt, b_r)

    counts = counts[:, 0]
    padded = (counts + MOE_BLOCK - 1) // MOE_BLOCK * MOE_BLOCK
    pad_end = jnp.cumsum(padded)
    pad_start = pad_end - padded
    n_blocks = (t * TOP_K + N_EXPERTS * (MOE_BLOCK - 1) + MOE_BLOCK - 1) // MOE_BLOCK
    expert_ids = jnp.arange(N_EXPERTS, dtype=jnp.int32)
    run_start = jnp.sum(jnp.where(eidx[..., None] == expert_ids, pad_start, 0), axis=-1)
    dest = (run_start + rank).reshape(-1).astype(jnp.int32)
    block_start = jnp.arange(n_blocks, dtype=jnp.int32) * MOE_BLOCK
    block_e = jnp.minimum(jnp.sum((pad_end[None, :] <= block_start[:, None]).astype(jnp.int32), axis=1),
                          N_EXPERTS - 1)
    first = jnp.concatenate([jnp.ones((1,), jnp.int32), (block_e[1:] != block_e[:-1]).astype(jnp.int32)])
    run = jnp.cumsum(first) - 1
    later_nonempty = (expert_ids[None, :] > expert_ids[:, None]) & (padded[None, :] > 0)
    next_of = jnp.min(jnp.where(later_nonempty, expert_ids[None, :], N_EXPERTS), axis=1)
    next_e = jnp.sum(jnp.where(block_e[:, None] == expert_ids, next_of, 0), axis=-1)
    n_used = (pad_end[-1:] // MOE_BLOCK).astype(jnp.int32)
    dense_start = jnp.cumsum(counts) - counts
    dense_pos = jnp.sum(jnp.where(eidx[..., None] == expert_ids, dense_start, 0), axis=-1) + rank
    order_tok = (jnp.argsort(dense_pos.reshape(-1)) % t).astype(jnp.int32)
    gap = pad_start - dense_start
    block_base = block_start - jnp.sum(jnp.where(block_e[:, None] == expert_ids, gap, 0), axis=-1)

    yb = moe_experts(order_tok, block_base.astype(jnp.int32), block_e, first, run.astype(jnp.int32),
                     next_e.astype(jnp.int32), n_used, hn_packed, w_gate, w_up, w_down, layer, n_blocks)
    return moe_combine_ple(dest, h, gates.T, p_i, yb, ple_w_gate.astype(BF16), ple_w_proj.astype(BF16), ple_norm)


def _lru_layer(h, batch, norm, w_in, conv_w, conv_b, w_rg, b_rg, w_ig, b_ig, lam, w_out):
    w_gates = jnp.concatenate([w_rg, w_ig], axis=-1).astype(BF16)
    return lru_core(h, norm, w_in.astype(BF16), conv_w, conv_b, w_gates, b_rg.reshape(-1), b_ig.reshape(-1), lam,
                    w_out.astype(BF16), batch=batch)


def _attn_layer(h, batch, cos, sin, norm, w_qkv, q_gain, k_gain, w_o):
    w_qkv = w_qkv.astype(BF16)
    outs, lses = [], []
    for g, (_, dilation) in enumerate(DILATION_PAIRS):
        qkv_g = qkv_project(h, norm, w_qkv, cos, sin, q_gain[g], k_gain[g], g, dilation, batch=batch)
        o, lse = dilated_attention(qkv_g)
        outs.append(o)
        lses.append(lse)
    return merge_project(outs, lses, h, w_o.astype(BF16), batch=batch)


def kernel(x, p, positions, norm_mix, norm_ffn, lru_w_in, lru_conv_w, lru_conv_b, lru_w_rg, lru_b_rg, lru_w_ig, lru_b_ig, lru_lambda, lru_w_out, attn_w_qkv, attn_q_gain, attn_k_gain, attn_w_o, moe_w_group, moe_b_group, moe_w_expert, moe_b_expert, moe_w_gate, moe_w_up, moe_w_down, ple_w_proj, ple_w_gate, ple_norm):
    batch, seq, d = x.shape
    depth = p.shape[0]
    t = batch * seq
    h = x.reshape(t, d)
    cos, sin = rope_tables(positions.reshape(t))
    for i in range(depth):
        j = i // 2
        if i % 2 == 0:
            h = _lru_layer(h, batch, norm_mix[i], lru_w_in[j], lru_conv_w[j], lru_conv_b[j], lru_w_rg[j],
                           lru_b_rg[j], lru_w_ig[j], lru_b_ig[j], lru_lambda[j], lru_w_out[j])
        else:
            h = _attn_layer(h, batch, cos, sin, norm_mix[i], attn_w_qkv[j], attn_q_gain[j], attn_k_gain[j],
                            attn_w_o[j])
        h = _moe_ple_layer(h, p[i].reshape(t, -1), norm_ffn[i], moe_w_group[i], moe_b_group[i],
                           moe_w_expert[i], moe_b_expert[i], moe_w_gate, moe_w_up, moe_w_down, i,
                           ple_w_proj[i], ple_w_gate[i], ple_norm[i])
    return h.reshape(batch, seq, d)
```

```python
import functools

import jax
import jax.numpy as jnp
from jax import lax
from jax.experimental import pallas as pl
from jax.experimental.pallas import tpu as pltpu

NORM_EPS = 1e-6
N_LRU_BLOCKS = 8
CONV_WIDTH = 4
LRU_C = 8.0
DILATION_PAIRS = ((128, 1), (512, 4), (2048, 16))
N_HEADS = 8
HEAD_DIM = 128
ATTN_BLOCK = 128
ROPE_THETA = 10000.0
NEG_INF = -1e30
N_GROUPS = 8
EXPERTS_PER_GROUP = 8
N_EXPERTS = N_GROUPS * EXPERTS_PER_GROUP
TOP_K = 2

LANES = 128
SUBLANES = 8
VMEM_LIMIT_BYTES = 56 * 1024 * 1024

ROW_TILE = 512
LRU_CHUNK = 256
ATTN_TILE = 512
MOE_BLOCK = 256
EXPERT_AHEAD = 3
WEIGHT_DMA_PRIORITY = 1
QKV_TILE = 256
QKV_CHUNK = 256
MERGE_TILE = 256
DMA_UNROLL = 8
COMBINE_SUB = 256
COMBINE_SUBS = 2
COMBINE_AHEAD = 2

F32 = jnp.float32
BF16 = jnp.bfloat16


def _cparams(*sem):
    return pltpu.CompilerParams(dimension_semantics=sem, vmem_limit_bytes=VMEM_LIMIT_BYTES)


def _rms(x, gain):
    ms = jnp.mean(x * x, axis=-1, keepdims=True)
    return x * lax.rsqrt(ms + NORM_EPS) * gain


def _pack_bf16_pairs(x):
    half = x.shape[1] // 2
    lo = pltpu.bitcast(x[:, :half].astype(BF16).astype(F32), jnp.uint32)
    hi = pltpu.bitcast(x[:, half:].astype(BF16).astype(F32), jnp.uint32)
    return hi | (lo >> 16)


def _unpack_bf16_pairs(xw):
    lo = pltpu.bitcast(xw << 16, F32)
    hi = pltpu.bitcast(xw & jnp.uint32(0xFFFF0000), F32)
    return jnp.concatenate([lo, hi], axis=1)


def _sigmoid(x):
    return 0.5 * jnp.tanh(0.5 * x) + 0.5


def _shift_rows_in_vreg(x, s, fill):
    g = x.shape[0] // SUBLANES
    x3 = x.reshape(g, SUBLANES, x.shape[1])
    rolled = pltpu.roll(x3, s, axis=1)
    row = lax.broadcasted_iota(jnp.int32, x3.shape, 1)
    return jnp.where(row >= s, rolled, fill).reshape(x.shape)


def _shift_rows(x, s, fill):
    row = lax.broadcasted_iota(jnp.int32, x.shape, 0)
    return jnp.where(row >= s, pltpu.roll(x, s, axis=0), fill)


def _one_minus_exp(x, ex):
    near = jnp.where(ex == 1.0, -x, (1.0 - ex) * x / jnp.log(ex))
    return jnp.where(x > -0.5, near, 1.0 - ex)


def _lru_kernel(h_ref, gmix_ref, win_ref, cw_ref, cb_ref, wg_ref, brg_ref, big_ref, lam_ref, wo_ref,
                o_ref, xn_ref, xprev_ref, hcarry_ref, xc_ref, a_ref, u_ref, hin_ref, tot_ref):
    tc, d = h_ref.shape
    blk = d // N_LRU_BLOCKS

    @pl.when(pl.program_id(1) == 0)
    def _():
        xprev_ref[...] = jnp.zeros_like(xprev_ref)
        hcarry_ref[...] = jnp.zeros_like(hcarry_ref)

    xn_ref[...] = _rms(h_ref[...], gmix_ref[...]).astype(BF16)
    x = jnp.dot(xn_ref[...], win_ref[:, d:], preferred_element_type=F32)

    prev = xprev_ref[...]
    cw = cw_ref[...]
    acc = cb_ref[...] + cw[CONV_WIDTH - 1:CONV_WIDTH] * x
    head = cb_ref[...] + cw[CONV_WIDTH - 1:CONV_WIDTH] * x[0:SUBLANES]
    row8 = lax.broadcasted_iota(jnp.int32, (SUBLANES, d), 0)
    for s in range(1, CONV_WIDTH):
        wk = cw[CONV_WIDTH - 1 - s:CONV_WIDTH - s]
        acc = acc + wk * pltpu.roll(x, s, axis=0)
        first = jnp.where(row8 >= s, pltpu.roll(x[0:SUBLANES], s, axis=0), pltpu.roll(prev, s, axis=0))
        head = head + wk * first
    xc_ref[...] = acc
    xc_ref[0:SUBLANES, :] = head
    xprev_ref[...] = x[tc - SUBLANES:tc]

    lam = lam_ref[...]
    log_sig = jnp.minimum(lam, 0.0) - jnp.log1p(jnp.exp(-jnp.abs(lam)))

    for n in range(N_LRU_BLOCKS):
        sl = slice(n * blk, (n + 1) * blk)
        xc = xc_ref[:, sl]
        z = jnp.dot(xc.astype(BF16), wg_ref[n], preferred_element_type=F32)
        r = _sigmoid(z[:, :blk] + brg_ref[:, sl])
        i = _sigmoid(z[:, blk:] + big_ref[:, sl])
        log_a = LRU_C * r * log_sig[:, sl]
        a = jnp.exp(log_a)
        a_ref[:, sl] = a
        u_ref[:, sl] = jnp.sqrt(_one_minus_exp(2.0 * log_a, a * a)) * (i * xc)

    a = a_ref[...]
    u = u_ref[...]
    for s in (1, 2, 4):
        a_sh = _shift_rows_in_vreg(a, s, 1.0)
        u_sh = _shift_rows_in_vreg(u, s, 0.0)
        u = a * u_sh + u
        a = a * a_sh
    a_ref[...] = a
    u_ref[...] = u
    ng = tc // SUBLANES
    for j in range(ng):
        last = slice(j * SUBLANES + SUBLANES - 1, (j + 1) * SUBLANES)
        hin_ref[j:j + 1, :] = a_ref[last, :]
        tot_ref[j:j + 1, :] = u_ref[last, :]
    pa = hin_ref[...]
    su = tot_ref[...]
    s = 1
    while s < ng:
        su = pa * _shift_rows(su, s, 0.0) + su
        pa = pa * _shift_rows(pa, s, 1.0)
        s *= 2
    h0 = hcarry_ref[...]
    after = pa * h0 + su
    hcarry_ref[...] = after[ng - 1:ng]
    row = lax.broadcasted_iota(jnp.int32, after.shape, 0)
    hin_ref[...] = jnp.where(row == 0, h0, pltpu.roll(after, 1, axis=0))
    for j in range(ng):
        rows = slice(j * SUBLANES, (j + 1) * SUBLANES)
        h_in = jnp.broadcast_to(hin_ref[j:j + 1, :], (SUBLANES, d))
        u_ref[rows, :] = a_ref[rows, :] * h_in + u_ref[rows, :]

    gate = jnp.dot(xn_ref[...], win_ref[:, :d], preferred_element_type=F32)
    y = jax.nn.gelu(gate, approximate=True) * u_ref[...]
    o_ref[...] = h_ref[...] + jnp.dot(y.astype(BF16), wo_ref[...], preferred_element_type=F32)


def lru_core(h, gain, w_in, conv_w, conv_b, w_gates, b_rg, b_ig, lam, w_out, *, batch):
    t, d = h.shape
    seq = t // batch
    tc = min(LRU_CHUNK, seq)
    nc = seq // tc
    blk = d // N_LRU_BLOCKS
    row = lambda b, c: (b * nc + c, 0)
    const2 = lambda b, c: (0, 0)
    resident = pl.Buffered(1)
    return pl.pallas_call(
        _lru_kernel,
        grid=(batch, nc),
        in_specs=[
            pl.BlockSpec((tc, d), row),
            pl.BlockSpec((1, d), const2),
            pl.BlockSpec((d, 2 * d), const2, pipeline_mode=resident),
            pl.BlockSpec((CONV_WIDTH, d), const2),
            pl.BlockSpec((1, d), const2),
            pl.BlockSpec((N_LRU_BLOCKS, blk, 2 * blk), lambda b, c: (0, 0, 0), pipeline_mode=resident),
            pl.BlockSpec((1, d), const2),
            pl.BlockSpec((1, d), const2),
            pl.BlockSpec((1, d), const2),
            pl.BlockSpec((d, d), const2, pipeline_mode=resident),
        ],
        out_specs=pl.BlockSpec((tc, d), row),
        out_shape=jax.ShapeDtypeStruct((t, d), F32),
        scratch_shapes=[
            pltpu.VMEM((tc, d), BF16),
            pltpu.VMEM((SUBLANES, d), F32),
            pltpu.VMEM((1, d), F32),
            pltpu.VMEM((tc, d), F32),
            pltpu.VMEM((tc, d), F32),
            pltpu.VMEM((tc, d), F32),
            pltpu.VMEM((tc // SUBLANES, d), F32),
            pltpu.VMEM((tc // SUBLANES, d), F32),
        ],
        compiler_params=_cparams("parallel", "arbitrary"),
        name="lru_core",
    )(h, gain.reshape(1, d), w_in, conv_w, conv_b.reshape(1, d), w_gates, b_rg.reshape(1, d), b_ig.reshape(1, d),
      lam.reshape(1, d), w_out)


def _rope_kernel(pos_ref, inv_ref, cos_ref, sin_ref):
    ang = pos_ref[...].astype(F32) * inv_ref[...]
    lane = lax.broadcasted_iota(jnp.int32, ang.shape, 1)
    cos_ref[...] = jnp.cos(ang)
    sin_ref[...] = jnp.where(lane < HEAD_DIM // 2, -jnp.sin(ang), jnp.sin(ang))


def rope_tables(positions_flat):
    t = positions_flat.shape[0]
    tm = min(ROW_TILE, t)
    inv = ROPE_THETA ** (-jnp.arange(0, HEAD_DIM, 2, dtype=F32) / HEAD_DIM)
    inv2 = jnp.concatenate([inv, inv]).reshape(1, HEAD_DIM)
    return pl.pallas_call(
        _rope_kernel,
        grid=(t // tm,),
        in_specs=[pl.BlockSpec((tm, 1), lambda i: (i, 0)), pl.BlockSpec((1, HEAD_DIM), lambda i: (0, 0))],
        out_specs=[pl.BlockSpec((tm, HEAD_DIM), lambda i: (i, 0))] * 2,
        out_shape=[jax.ShapeDtypeStruct((t, HEAD_DIM), F32)] * 2,
        compiler_params=_cparams("parallel"),
        name="rope_tables",
    )(positions_flat.reshape(t, 1), inv2)


def _qkv_kernel(x_ref, g_ref, w_ref, cos_ref, sin_ref, qg_ref, kg_ref, o_ref, stage_ref, stage2_ref, *, dilation):
    assert dilation in (1, 4, 16)
    tm = x_ref.shape[0]
    hd = N_HEADS * HEAD_DIM
    xn = _rms(x_ref[...], g_ref[...]).astype(BF16)
    cos = cos_ref[...]
    sin = sin_ref[...]
    for kind in range(3):
        for c in range(hd // QKV_CHUNK):
            cols = slice(kind * hd + c * QKV_CHUNK, kind * hd + (c + 1) * QKV_CHUNK)
            z = jnp.dot(xn, w_ref[:, cols], preferred_element_type=F32)
            if kind < 2:
                gain = (qg_ref if kind == 0 else kg_ref)[...]
                heads = []
                for hh in range(QKV_CHUNK // HEAD_DIM):
                    y = _rms(z[:, hh * HEAD_DIM:(hh + 1) * HEAD_DIM], gain)
                    y = y * cos + pltpu.roll(y, HEAD_DIM // 2, axis=1) * sin
                    heads.append(y * (HEAD_DIM ** -0.5) if kind == 0 else y)
                z = jnp.concatenate(heads, axis=1)
            if dilation == 1:
                o_ref[0, :, cols] = z.astype(BF16)
            else:
                for hh in range(QKV_CHUNK // LANES):
                    stage_ref[cols.start // LANES + hh] = z[:, hh * LANES:(hh + 1) * LANES]
    if dilation == 4:
        for r in range(dilation):
            for cc in range(3 * hd // LANES):
                o_ref[r, :, cc * LANES:(cc + 1) * LANES] = (
                    stage_ref[cc, pl.ds(r, tm // dilation, stride=dilation), :].astype(BF16))
    if dilation == 16:
        quarter = tm // 4
        for cc in range(3 * hd // LANES):
            for r1 in range(4):
                stage2_ref[cc, r1 * quarter:(r1 + 1) * quarter, :] = stage_ref[cc, pl.ds(r1, quarter, stride=4), :]
            for r1 in range(4):
                for r2 in range(4):
                    o_ref[r1 + 4 * r2, :, cc * LANES:(cc + 1) * LANES] = (
                        stage2_ref[cc, pl.ds(r1 * quarter + r2, tm // dilation, stride=4), :].astype(BF16))


def qkv_project(x, gain, w_qkv, cos, sin, q_gain, k_gain, group, dilation, *, batch):
    t, k = x.shape
    seq = t // batch
    sub = seq // dilation
    hd3 = 3 * N_HEADS * HEAD_DIM
    tm = min(QKV_TILE, seq)
    per_batch = seq // tm
    row = lambda i: (i, 0)
    const = lambda i: (0, 0)
    return pl.pallas_call(
        functools.partial(_qkv_kernel, dilation=dilation),
        grid=(t // tm,),
        in_specs=[
            pl.BlockSpec((tm, k), row),
            pl.BlockSpec((1, k), const),
            pl.BlockSpec((k, hd3), lambda i: (0, group)),
            pl.BlockSpec((tm, HEAD_DIM), row),
            pl.BlockSpec((tm, HEAD_DIM), row),
            pl.BlockSpec((1, HEAD_DIM), const),
            pl.BlockSpec((1, HEAD_DIM), const),
        ],
        out_specs=pl.BlockSpec((None, dilation, tm // dilation, hd3),
                               lambda i: (i // per_batch, 0, i % per_batch, 0)),
        out_shape=jax.ShapeDtypeStruct((batch, dilation, sub, hd3), BF16),
        scratch_shapes=[pltpu.VMEM((hd3 // LANES, tm if dilation > 1 else SUBLANES, LANES), F32),
                        pltpu.VMEM((hd3 // LANES, tm if dilation == 16 else SUBLANES, LANES), F32)],
        compiler_params=_cparams("parallel"),
        name=f"qkv_project_d{dilation}",
    )(x, gain.reshape(1, k), w_qkv, cos, sin, q_gain.reshape(1, HEAD_DIM), k_gain.reshape(1, HEAD_DIM))


def _attn_kernel(q_ref, kc_ref, kp_ref, vc_ref, vp_ref, o_ref, lse_ref, kbuf_ref, vbuf_ref):
    tl = q_ref.shape[0]
    blk = ATTN_BLOCK
    first_tile = pl.program_id(2) == 0

    kbuf_ref[0:blk, :] = kp_ref[...]
    kbuf_ref[blk:, :] = kc_ref[...]
    vbuf_ref[0:blk, :] = vp_ref[...]
    vbuf_ref[blk:, :] = vc_ref[...]

    qi = lax.broadcasted_iota(jnp.int32, (blk, 2 * blk), 0)
    kj = lax.broadcasted_iota(jnp.int32, (blk, 2 * blk), 1)
    upper = kj <= qi + blk
    bias_gen = jnp.where(upper & (kj >= qi), 0.0, NEG_INF).astype(F32)
    bias_first = jnp.where(upper & (kj >= blk), 0.0, NEG_INF).astype(F32)
    bias0 = jnp.where(first_tile, bias_first, bias_gen)
    lane = lax.broadcasted_iota(jnp.int32, (blk, LANES), 1)

    for qb in range(tl // blk):
        bias = bias0 if qb == 0 else bias_gen
        rows = slice(qb * blk, (qb + 1) * blk)
        win = slice(qb * blk, (qb + 2) * blk)
        lse_tile = jnp.zeros((blk, LANES), F32)
        for h in range(N_HEADS):
            cols = slice(h * HEAD_DIM, (h + 1) * HEAD_DIM)
            s = lax.dot_general(q_ref[rows, cols], kbuf_ref[win, cols], (((1,), (1,)), ((), ())),
                                preferred_element_type=F32) + bias
            m = jnp.max(s, axis=-1, keepdims=True)
            p = jnp.exp(s - m)
            den = jnp.sum(p, axis=-1, keepdims=True)
            o = jnp.dot(p.astype(BF16), vbuf_ref[win, cols], preferred_element_type=F32)
            o_ref[rows, cols] = o / den
            lse_tile = jnp.where(lane == h, m + jnp.log(den), lse_tile)
        lse_ref[rows, :] = lse_tile


def dilated_attention(qkv_g):
    batch, dilation, sub, hd3 = qkv_g.shape
    hd = hd3 // 3
    tl = min(ATTN_TILE, sub)
    per = tl // ATTN_BLOCK
    cur = lambda kind: (lambda b, r, i: (b, r, i, kind))
    prev = lambda kind: (lambda b, r, i: (b, r, jnp.maximum(i * per - 1, 0), kind))
    return pl.pallas_call(
        _attn_kernel,
        grid=(batch, dilation, sub // tl),
        in_specs=[
            pl.BlockSpec((None, None, tl, hd), cur(0)),
            pl.BlockSpec((None, None, tl, hd), cur(1)),
            pl.BlockSpec((None, None, ATTN_BLOCK, hd), prev(1)),
            pl.BlockSpec((None, None, tl, hd), cur(2)),
            pl.BlockSpec((None, None, ATTN_BLOCK, hd), prev(2)),
        ],
        out_specs=[
            pl.BlockSpec((None, None, tl, hd), lambda b, r, i: (b, r, i, 0)),
            pl.BlockSpec((None, None, tl, LANES), lambda b, r, i: (b, r, i, 0)),
        ],
        out_shape=[
            jax.ShapeDtypeStruct((batch, dilation, sub, hd), F32),
            jax.ShapeDtypeStruct((batch, dilation, sub, LANES), F32),
        ],
        scratch_shapes=[pltpu.VMEM((tl + ATTN_BLOCK, hd), BF16), pltpu.VMEM((tl + ATTN_BLOCK, hd), BF16)],
        compiler_params=_cparams("parallel", "parallel", "arbitrary"),
        name=f"dilated_attention_d{dilation}",
    )(qkv_g, qkv_g, qkv_g, qkv_g, qkv_g)


def _interleave(src_ref, dst_ref, tmp_ref):
    dilation, rows, width = src_ref.shape
    assert dilation in (1, 4, 16)
    for c in range(width // LANES):
        cols = slice(c * LANES, (c + 1) * LANES)
        if dilation == 1:
            dst_ref[c] = src_ref[0, :, cols]
        elif dilation == 4:
            for r in range(dilation):
                dst_ref[c, pl.ds(r, rows, stride=dilation), :] = src_ref[r, :, cols]
        else:
            quarter = 4 * rows
            for r1 in range(4):
                for r2 in range(4):
                    tmp_ref[pl.ds(r1 * quarter + r2, rows, stride=4), :] = src_ref[r1 + 4 * r2, :, cols]
            for r1 in range(4):
                dst_ref[c, pl.ds(r1, quarter, stride=4), :] = tmp_ref[r1 * quarter:(r1 + 1) * quarter, :]


def _merge_kernel(o0_ref, o1_ref, o2_ref, l0_ref, l1_ref, l2_ref, hres_ref, wo_ref, out_ref, obuf, lbuf, tmp):
    for g, (o_ref, l_ref) in enumerate(((o0_ref, l0_ref), (o1_ref, l1_ref), (o2_ref, l2_ref))):
        _interleave(o_ref, obuf.at[g], tmp)
        _interleave(l_ref, lbuf.at[g], tmp)
    l0, l1, l2 = lbuf[0, 0], lbuf[1, 0], lbuf[2, 0]
    m = jnp.maximum(jnp.maximum(l0, l1), l2)
    e0, e1, e2 = jnp.exp(l0 - m), jnp.exp(l1 - m), jnp.exp(l2 - m)
    tot = e0 + e1 + e2
    w0, w1, w2 = e0 / tot, e1 / tot, e2 / tot
    tm = l0.shape[0]
    parts = []
    for h in range(N_HEADS):
        bc = lambda w: jnp.broadcast_to(w[:, h:h + 1], (tm, HEAD_DIM))
        parts.append(bc(w0) * obuf[0, h] + bc(w1) * obuf[1, h] + bc(w2) * obuf[2, h])
    o = jnp.concatenate(parts, axis=1).astype(BF16)
    out_ref[...] = hres_ref[...] + jnp.dot(o, wo_ref[...], preferred_element_type=F32)


def merge_project(outs, lses, h_res, w_o, *, batch):
    t, d = h_res.shape
    seq = t // batch
    hd = N_HEADS * HEAD_DIM
    tm = min(MERGE_TILE, seq)
    per_batch = seq // tm
    row = lambda i: (i, 0)

    def grouped(a):
        dil, width = a.shape[1], a.shape[3]
        return pl.BlockSpec((None, dil, tm // dil, width), lambda i: (i // per_batch, 0, i % per_batch, 0))

    return pl.pallas_call(
        _merge_kernel,
        grid=(t // tm,),
        in_specs=[grouped(a) for a in outs] + [grouped(a) for a in lses] + [
            pl.BlockSpec((tm, d), row),
            pl.BlockSpec((hd, d), lambda i: (0, 0)),
        ],
        out_specs=pl.BlockSpec((tm, d), row),
        out_shape=jax.ShapeDtypeStruct((t, d), F32),
        scratch_shapes=[pltpu.VMEM((len(outs), hd // LANES, tm, LANES), F32),
                        pltpu.VMEM((len(lses), 1, tm, LANES), F32),
                        pltpu.VMEM((tm, LANES), F32)],
        compiler_params=_cparams("parallel"),
        name="merge_project",
    )(*outs, *lses, h_res, w_o)


def _router_kernel(h_ref, g_ref, wr_ref, br_ref, hn_ref, eidx_ref, gate_ref, rank_ref, cnt_ref, carry_ref):
    tm = h_ref.shape[0]

    @pl.when(pl.program_id(0) == 0)
    def _():
        carry_ref[...] = jnp.zeros_like(carry_ref)

    hn = _rms(h_ref[...], g_ref[...])
    hn_ref[...] = _pack_bf16_pairs(hn)

    logits = lax.dot_general(wr_ref[...], hn, (((1,), (1,)), ((), ())), precision=lax.Precision.HIGHEST,
                             preferred_element_type=F32) + br_ref[:, 0:1]
    lg = logits[0:N_GROUPS]
    gi = lax.broadcasted_iota(jnp.int32, (N_GROUPS, tm), 0)
    m = jnp.max(lg, axis=0, keepdims=True)
    p_top = 1.0 / jnp.sum(jnp.exp(lg - m), axis=0, keepdims=True)
    g_top = jnp.min(jnp.where(lg == m, gi, N_GROUPS), axis=0, keepdims=True)

    le = jnp.zeros((EXPERTS_PER_GROUP, tm), F32)
    for g in range(N_GROUPS):
        lo_row = N_GROUPS + g * EXPERTS_PER_GROUP
        le = jnp.where(g_top == g, logits[lo_row:lo_row + EXPERTS_PER_GROUP], le)
    ei = lax.broadcasted_iota(jnp.int32, (EXPERTS_PER_GROUP, tm), 0)
    v1 = jnp.max(le, axis=0, keepdims=True)
    i1 = jnp.min(jnp.where(le == v1, ei, EXPERTS_PER_GROUP), axis=0, keepdims=True)
    le2 = jnp.where(ei == i1, -jnp.inf, le)
    v2 = jnp.max(le2, axis=0, keepdims=True)
    i2 = jnp.min(jnp.where(le2 == v2, ei, EXPERTS_PER_GROUP), axis=0, keepdims=True)
    e21 = jnp.exp(v2 - v1)
    gate1 = p_top / (1.0 + e21)
    gate2 = p_top * e21 / (1.0 + e21)
    e1 = g_top * EXPERTS_PER_GROUP + i1
    e2 = g_top * EXPERTS_PER_GROUP + i2
    eidx_ref[...] = jnp.concatenate([e1, e2], axis=0)
    gate_ref[...] = jnp.concatenate([gate1, gate2], axis=0)

    ee = lax.broadcasted_iota(jnp.int32, (N_EXPERTS, tm), 0)
    oh1 = ee == e1
    oh2 = ee == e2
    onehot = jnp.where(oh1 | oh2, 1.0, 0.0)
    tri = jnp.where(lax.broadcasted_iota(jnp.int32, (tm, tm), 0) < lax.broadcasted_iota(jnp.int32, (tm, tm), 1),
                    1.0, 0.0).astype(BF16)
    before = jnp.dot(onehot.astype(BF16), tri, preferred_element_type=F32) + carry_ref[:, 0:1]
    r1 = jnp.sum(jnp.where(oh1, before, 0.0), axis=0, keepdims=True)
    r2 = jnp.sum(jnp.where(oh2, before, 0.0), axis=0, keepdims=True)
    rank_ref[...] = jnp.concatenate([r1, r2], axis=0).astype(jnp.int32)
    carry_ref[...] = carry_ref[...] + jnp.sum(onehot, axis=1, keepdims=True)
    cnt_ref[...] = carry_ref[...].astype(jnp.int32)


def moe_router(h, gain, w_router_t, b_router):
    t, d = h.shape
    tm = min(ROW_TILE, t)
    nlog = w_router_t.shape[0]
    return pl.pallas_call(
        _router_kernel,
        grid=(t // tm,),
        in_specs=[
            pl.BlockSpec((tm, d), lambda i: (i, 0)),
            pl.BlockSpec((1, d), lambda i: (0, 0)),
            pl.BlockSpec((nlog, d), lambda i: (0, 0)),
            pl.BlockSpec((nlog, LANES), lambda i: (0, 0)),
        ],
        out_specs=[
            pl.BlockSpec((tm, d // 2), lambda i: (i, 0)),
            pl.BlockSpec((TOP_K, tm), lambda i: (0, i)),
            pl.BlockSpec((TOP_K, tm), lambda i: (0, i)),
            pl.BlockSpec((TOP_K, tm), lambda i: (0, i)),
            pl.BlockSpec((N_EXPERTS, LANES), lambda i: (0, 0)),
        ],
        out_shape=[
            jax.ShapeDtypeStruct((t, d // 2), jnp.uint32),
            jax.ShapeDtypeStruct((TOP_K, t), jnp.int32),
            jax.ShapeDtypeStruct((TOP_K, t), F32),
            jax.ShapeDtypeStruct((TOP_K, t), jnp.int32),
            jax.ShapeDtypeStruct((N_EXPERTS, LANES), jnp.int32),
        ],
        scratch_shapes=[pltpu.VMEM((N_EXPERTS, LANES), F32)],
        compiler_params=_cparams("arbitrary"),
        name="moe_router",
    )(h, gain.reshape(1, d), w_router_t, b_router)


def _expert_kernel(tok_ref, base_ref, be_ref, first_ref, run_ref, next_ref, used_ref,
                   hn_hbm, wg_hbm, wu_hbm, wd_hbm, o_ref,
                   xbuf, wg_f32, wu_f32, wd_f32, wg_bf, wu_bf, wd_bf, sems, xsems, *, layer):
    i = pl.program_id(0)
    n = pl.num_programs(0)
    rows = xbuf.shape[1]
    n_slot = xbuf.shape[0]
    last_tok = tok_ref.shape[0] - 1
    active = i < used_ref[0]

    def row_copy(blk, slot, r):
        tok = tok_ref[jnp.minimum(base_ref[blk] + r, last_tok)]
        return pltpu.make_async_copy(hn_hbm.at[pl.ds(tok, 1)], xbuf.at[slot, pl.ds(r, 1)], xsems.at[slot])

    def start_rows_loop(blk, slot):
        def body(r, c):
            row_copy(blk, slot, r).start()
            return c
        lax.fori_loop(0, rows, body, 0, unroll=DMA_UNROLL)

    def wait_rows(blk, slot):
        def body(r, c):
            row_copy(blk, slot, r).wait()
            return c
        lax.fori_loop(0, rows, body, 0, unroll=DMA_UNROLL)

    def weight_copies(e, slot):
        return (pltpu.make_async_copy(wg_hbm.at[layer, e], wg_f32.at[slot], sems.at[slot, 0]),
                pltpu.make_async_copy(wu_hbm.at[layer, e], wu_f32.at[slot], sems.at[slot, 1]),
                pltpu.make_async_copy(wd_hbm.at[layer, e], wd_f32.at[slot], sems.at[slot, 2]))

    ahead = jnp.minimum(i + EXPERT_AHEAD, n - 1)
    ahead_slot = (i + EXPERT_AHEAD) % n_slot
    slot_x = i % n_slot

    @pl.when(i == 0)
    def _():
        for c in weight_copies(be_ref[0], 0):
            c.start(priority=WEIGHT_DMA_PRIORITY)
        for s in range(EXPERT_AHEAD):
            start_rows_loop(s, s)

    @pl.when(jnp.logical_and(active, first_ref[i] == 1))
    def _():
        slot = run_ref[i] % 2
        nxt = next_ref[i]

        @pl.when(nxt < N_EXPERTS)
        def _():
            for c in weight_copies(nxt, 1 - slot):
                c.start(priority=WEIGHT_DMA_PRIORITY)

        for c in weight_copies(be_ref[i], slot):
            c.wait()
        wg_bf[...] = wg_f32[slot].astype(BF16)
        wu_bf[...] = wu_f32[slot].astype(BF16)
        wd_bf[...] = wd_f32[slot].astype(BF16)

    wait_rows(i, slot_x)

    @pl.when(active)
    def _():
        x = _unpack_bf16_pairs(xbuf[slot_x]).astype(BF16)
        for r in range(rows):
            row_copy(ahead, ahead_slot, r).start()
        g = jnp.dot(x, wg_bf[...], preferred_element_type=F32)
        u = jnp.dot(x, wu_bf[...], preferred_element_type=F32)
        a = (g * _sigmoid(g) * u).astype(BF16)
        o_ref[...] = _pack_bf16_pairs(jnp.dot(a, wd_bf[...], preferred_element_type=F32))

    @pl.when(jnp.logical_not(active))
    def _():
        start_rows_loop(ahead, ahead_slot)
        o_ref[...] = jnp.zeros_like(o_ref)

    @pl.when(i == n - 1)
    def _():
        for s in range(EXPERT_AHEAD):
            wait_rows(n - 1, (n + s) % n_slot)


def moe_experts(order_tok, block_base, block_e, first, run, next_e, n_used, hn_packed, w_gate, w_up, w_down, layer,
                n_blocks):
    _, w = hn_packed.shape
    d = 2 * w
    de = w_gate.shape[3]
    return pl.pallas_call(
        functools.partial(_expert_kernel, layer=layer),
        grid_spec=pltpu.PrefetchScalarGridSpec(
            num_scalar_prefetch=7,
            grid=(n_blocks,),
            in_specs=[pl.BlockSpec(memory_space=pl.ANY)] * 4,
            out_specs=pl.BlockSpec((MOE_BLOCK, w), lambda i, *_: (i, 0)),
            scratch_shapes=[
                pltpu.VMEM((EXPERT_AHEAD + 1, MOE_BLOCK, w), jnp.uint32),
                pltpu.VMEM((2, d, de), F32), pltpu.VMEM((2, d, de), F32), pltpu.VMEM((2, de, d), F32),
                pltpu.VMEM((d, de), BF16), pltpu.VMEM((d, de), BF16), pltpu.VMEM((de, d), BF16),
                pltpu.SemaphoreType.DMA((2, 3)),
                pltpu.SemaphoreType.DMA((EXPERT_AHEAD + 1,)),
            ],
        ),
        out_shape=jax.ShapeDtypeStruct((n_blocks * MOE_BLOCK, w), jnp.uint32),
        compiler_params=_cparams("arbitrary"),
        name="moe_experts",
    )(order_tok, block_base, block_e, first, run, next_e, n_used, hn_packed, w_gate, w_up, w_down)


def _combine_kernel(dest_ref, h_ref, gates_ref, p_ref, y_hbm, wpg_ref, wpp_ref, pn_ref, o_ref, ybuf, sems):
    st = h_ref.shape[0] // COMBINE_SUBS
    t = dest_ref.shape[0] // TOP_K
    i = pl.program_id(0)
    n_sub = pl.num_programs(0) * COMBINE_SUBS
    n_slot = ybuf.shape[0]

    def row_copy(sub, slot, r, k):
        d = dest_ref[k * t + sub * st + r]
        return pltpu.make_async_copy(y_hbm.at[pl.ds(d, 1)], ybuf.at[slot, k, pl.ds(r, 1)], sems.at[slot])

    def start_sub_loop(sub, slot):
        def body(r, c):
            for k in range(TOP_K):
                row_copy(sub, slot, r, k).start()
            return c
        lax.fori_loop(0, st, body, 0, unroll=DMA_UNROLL)

    def wait_sub(sub, slot):
        def body(r, c):
            for k in range(TOP_K):
                row_copy(sub, slot, r, k).wait()
            return c
        lax.fori_loop(0, st, body, 0, unroll=DMA_UNROLL)

    @pl.when(i == 0)
    def _():
        for s in range(COMBINE_AHEAD):
            start_sub_loop(s, s)

    for j in range(COMBINE_SUBS):
        sub = i * COMBINE_SUBS + j
        slot = sub % n_slot
        wait_sub(sub, slot)
        rows = slice(j * st, (j + 1) * st)
        gates = gates_ref[rows, :]
        h = h_ref[rows, :] + (gates[:, 0:1] * _unpack_bf16_pairs(ybuf[slot, 0])
                              + gates[:, 1:2] * _unpack_bf16_pairs(ybuf[slot, 1]))
        ahead = jnp.minimum(sub + COMBINE_AHEAD, n_sub - 1)
        ahead_slot = (sub + COMBINE_AHEAD) % n_slot
        for r in range(st):
            for k in range(TOP_K):
                row_copy(ahead, ahead_slot, r, k).start()
        gate = _sigmoid(jnp.dot(h.astype(BF16), wpg_ref[...], preferred_element_type=F32))
        e = jnp.dot(p_ref[rows, :].astype(BF16), wpp_ref[...], preferred_element_type=F32)
        o_ref[rows, :] = h + _rms(e * gate, pn_ref[...])

    @pl.when(i == pl.num_programs(0) - 1)
    def _():
        for s in range(COMBINE_AHEAD):
            wait_sub(n_sub - 1, (n_sub + s) % n_slot)


def moe_combine_ple(dest_flat, h, gates_tk, p, yb, w_ple_gate, w_ple_proj, ple_gain):
    t, d = h.shape
    tm = min(COMBINE_SUB * COMBINE_SUBS, t)
    sub = tm // COMBINE_SUBS
    pd = p.shape[1]
    row = lambda i, dest: (i, 0)
    const = lambda i, dest: (0, 0)
    resident = pl.Buffered(1)
    return pl.pallas_call(
        _combine_kernel,
        grid_spec=pltpu.PrefetchScalarGridSpec(
            num_scalar_prefetch=1,
            grid=(t // tm,),
            in_specs=[
                pl.BlockSpec((tm, d), row),
                pl.BlockSpec((tm, TOP_K), row),
                pl.BlockSpec((tm, pd), row),
                pl.BlockSpec(memory_space=pl.ANY),
                pl.BlockSpec((d, d), const, pipeline_mode=resident),
                pl.BlockSpec((pd, d), const, pipeline_mode=resident),
                pl.BlockSpec((1, d), const),
            ],
            out_specs=pl.BlockSpec((tm, d), row),
            scratch_shapes=[pltpu.VMEM((COMBINE_SUBS + COMBINE_AHEAD, TOP_K, sub, yb.shape[1]), yb.dtype),
                            pltpu.SemaphoreType.DMA((COMBINE_SUBS + COMBINE_AHEAD,))],
        ),
        out_shape=jax.ShapeDtypeStruct((t, d), F32),
        compiler_params=_cparams("arbitrary"),
        name="moe_combine_ple",
    )(dest_flat, h, gates_tk, p, yb, w_ple_gate, w_ple_proj, ple_gain.reshape(1, d))


def _moe_ple_layer(h, p_i, norm_ffn, w_group, b_group, w_expert, b_expert, w_gate, w_up, w_down, layer,
                   ple_w_proj, ple_w_gate, ple_norm):
    t, d = h.shape
    ng, _, epg = w_expert.shape
    w_rt = jnp.concatenate([w_group.T, jnp.transpose(w_expert, (0, 2, 1)).reshape(ng * epg, d)], axis=0)
    b_r = jnp.concatenate([b_group, b_expert.reshape(-1)])
    pad = LANES - w_rt.shape[0]
    w_rt = jnp.pad(w_rt, ((0, pad), (0, 0)))
    b_r = jnp.broadcast_to(jnp.pad(b_r, (0, pad))[:, None], (LANES, LANES))

    hn_packed, eidx, gates, rank, counts = moe_router(h, norm_ffn, w_rt, b_r)

    counts = counts[:, 0]
    padded = (counts + MOE_BLOCK - 1) // MOE_BLOCK * MOE_BLOCK
    pad_end = jnp.cumsum(padded)
    pad_start = pad_end - padded
    n_blocks = (t * TOP_K + N_EXPERTS * (MOE_BLOCK - 1) + MOE_BLOCK - 1) // MOE_BLOCK
    expert_ids = jnp.arange(N_EXPERTS, dtype=jnp.int32)
    run_start = jnp.sum(jnp.where(eidx[..., None] == expert_ids, pad_start, 0), axis=-1)
    dest = (run_start + rank).reshape(-1).astype(jnp.int32)
    block_start = jnp.arange(n_blocks, dtype=jnp.int32) * MOE_BLOCK
    block_e = jnp.minimum(jnp.sum((pad_end[None, :] <= block_start[:, None]).astype(jnp.int32), axis=1),
                          N_EXPERTS - 1)
    first = jnp.concatenate([jnp.ones((1,), jnp.int32), (block_e[1:] != block_e[:-1]).astype(jnp.int32)])
    run = jnp.cumsum(first) - 1
    later_nonempty = (expert_ids[None, :] > expert_ids[:, None]) & (padded[None, :] > 0)
    next_of = jnp.min(jnp.where(later_nonempty, expert_ids[None, :], N_EXPERTS), axis=1)
    next_e = jnp.sum(jnp.where(block_e[:, None] == expert_ids, next_of, 0), axis=-1)
    n_used = (pad_end[-1:] // MOE_BLOCK).astype(jnp.int32)
    dense_start = jnp.cumsum(counts) - counts
    dense_pos = jnp.sum(jnp.where(eidx[..., None] == expert_ids, dense_start, 0), axis=-1) + rank
    order_tok = (jnp.argsort(dense_pos.reshape(-1)) % t).astype(jnp.int32)
    gap = pad_start - dense_start
    block_base = block_start - jnp.sum(jnp.where(block_e[:, None] == expert_ids, gap, 0), axis=-1)

    yb = moe_experts(order_tok, block_base.astype(jnp.int32), block_e, first, run.astype(jnp.int32),
                     next_e.astype(jnp.int32), n_used, hn_packed, w_gate, w_up, w_down, layer, n_blocks)
    return moe_combine_ple(dest, h, gates.T, p_i, yb, ple_w_gate.astype(BF16), ple_w_proj.astype(BF16), ple_norm)


def _lru_layer(h, batch, norm, w_in, conv_w, conv_b, w_rg, b_rg, w_ig, b_ig, lam, w_out):
    w_gates = jnp.concatenate([w_rg, w_ig], axis=-1).astype(BF16)
    return lru_core(h, norm, w_in.astype(BF16), conv_w, conv_b, w_gates, b_rg.reshape(-1), b_ig.reshape(-1), lam,
                    w_out.astype(BF16), batch=batch)


def _attn_layer(h, batch, cos, sin, norm, w_qkv, q_gain, k_gain, w_o):
    w_qkv = w_qkv.astype(BF16)
    outs, lses = [], []
    for g, (_, dilation) in enumerate(DILATION_PAIRS):
        qkv_g = qkv_project(h, norm, w_qkv, cos, sin, q_gain[g], k_gain[g], g, dilation, batch=batch)
        o, lse = dilated_attention(qkv_g)
        outs.append(o)
        lses.append(lse)
    return merge_project(outs, lses, h, w_o.astype(BF16), batch=batch)


def kernel(x, p, positions, norm_mix, norm_ffn, lru_w_in, lru_conv_w, lru_conv_b, lru_w_rg, lru_b_rg, lru_w_ig, lru_b_ig, lru_lambda, lru_w_out, attn_w_qkv, attn_q_gain, attn_k_gain, attn_w_o, moe_w_group, moe_b_group, moe_w_expert, moe_b_expert, moe_w_gate, moe_w_up, moe_w_down, ple_w_proj, ple_w_gate, ple_norm):
    batch, seq, d = x.shape
    depth = p.shape[0]
    t = batch * seq
    h = x.reshape(t, d)
    cos, sin = rope_tables(positions.reshape(t))
    for i in range(depth):
        j = i // 2
        if i % 2 == 0:
            h = _lru_layer(h, batch, norm_mix[i], lru_w_in[j], lru_conv_w[j], lru_conv_b[j], lru_w_rg[j],
                           lru_b_rg[j], lru_w_ig[j], lru_b_ig[j], lru_lambda[j], lru_w_out[j])
        else:
            h = _attn_layer(h, batch, cos, sin, norm_mix[i], attn_w_qkv[j], attn_q_gain[j], attn_k_gain[j],
                            attn_w_o[j])
        h = _moe_ple_layer(h, p[i].reshape(t, -1), norm_ffn[i], moe_w_group[i], moe_b_group[i],
                           moe_w_expert[i], moe_b_expert[i], moe_w_gate, moe_w_up, moe_w_down, i,
                           ple_w_proj[i], ple_w_gate[i], ple_norm[i])
    return h.reshape(batch, seq, d)
```
